```python
import jax, jax.numpy as jnp
from jax import lax
import numpy as np

D_MODEL = 1024
BATCH = 8
SEQ = 2048
DEPTH = 4
DEC_BATCH = 128
DEC_SEQ = 8
PAST_LEN = 2048
PAGE_SIZE = 128

N_HEADS = 8
HEAD_DIM = D_MODEL // 16
D_ATTN = N_HEADS * HEAD_DIM
MOBA_BLOCK = 256
MOBA_TOP_K = 3
Q_BLOCK = 128
GMLP_GROUPS = 4
D_GMLP = D_MODEL // 4
GMLP_CHUNK = 128
POOL_GROUPS = 4
D_POOL = D_MODEL // 4
POOL_WINDOWS = (2, 4, 8, 16)
POOL_BUF = max(POOL_WINDOWS) - 1
D_FF = 4 * D_MODEL
NORM_EPS = 1e-6
N_IN = 3 * D_ATTN + 2 * D_GMLP + D_POOL + 3 * D_MODEL
SPLITS = (D_ATTN, 2 * D_ATTN, 3 * D_ATTN, 3 * D_ATTN + D_GMLP,
          3 * D_ATTN + 2 * D_GMLP, 3 * D_ATTN + 2 * D_GMLP + D_POOL)

kernel_name = 'hybrid_moba_gmlp_pool_decoder_step'


def rms_norm(x, g):
    xf = x.astype(jnp.float32)
    y = xf * lax.rsqrt(jnp.mean(xf * xf, axis=-1, keepdims=True) + NORM_EPS)
    return (y * g.astype(jnp.float32)).astype(x.dtype)


def layer_norm_gain(x, g):
    xf = x.astype(jnp.float32)
    mu = jnp.mean(xf, axis=-1, keepdims=True)
    xc = xf - mu
    y = xc * lax.rsqrt(jnp.mean(xc * xc, axis=-1, keepdims=True) + NORM_EPS)
    return (y * g.astype(jnp.float32)).astype(x.dtype)


def moba_attention(q, k, v, q_start):
    b, t, h, hd = q.shape
    seq_len = k.shape[1]
    qpos = q_start + jnp.arange(t, dtype=jnp.int32)
    own_blk = qpos // MOBA_BLOCK
    n_full = seq_len // MOBA_BLOCK
    k_top = min(MOBA_TOP_K, n_full)
    own = jnp.broadcast_to(own_blk[None, :, None, None], (b, t, h, 1))
    if k_top > 0:
        kb = k[:, :n_full * MOBA_BLOCK].reshape(b, n_full, MOBA_BLOCK, h, hd)
        k_mean = jnp.mean(kb, axis=2, dtype=jnp.float32)
        gate = jnp.einsum('bthd,bnhd->bthn', q.astype(jnp.float32), k_mean)
        past = jnp.arange(n_full, dtype=jnp.int32)[None, :] < own_blk[:, None]
        gate = jnp.where(past[None, :, None, :], gate, -jnp.inf)
        top_val, top_idx = lax.top_k(gate, k_top)
        sel = jnp.concatenate([top_idx.astype(jnp.int32), own], axis=-1)
        valid = jnp.concatenate([jnp.isfinite(top_val), jnp.ones((b, t, h, 1), bool)], axis=-1)
    else:
        sel = own
        valid = jnp.ones((b, t, h, 1), bool)

    qb = min(Q_BLOCK, t)
    nqb = t // qb

    def blockify(a):
        return a.reshape((b * nqb, qb) + a.shape[2:])

    xs = (blockify(q), blockify(sel), blockify(valid),
          jnp.tile(qpos.reshape(nqb, qb), (b, 1)),
          jnp.repeat(jnp.arange(b, dtype=jnp.int32), nqb))
    offs = jnp.arange(MOBA_BLOCK, dtype=jnp.int32)
    h_idx = jnp.arange(h, dtype=jnp.int32)[None, :, None, None]
    scale = HEAD_DIM ** -0.5

    def attend(args):
        q_c, sel_c, valid_c, qpos_c, bi = args
        k_seq = k[bi]
        v_seq = v[bi]
        kpos = sel_c[..., None] * MOBA_BLOCK + offs
        gidx = jnp.minimum(kpos, seq_len - 1)
        kg = k_seq[gidx, h_idx]
        vg = v_seq[gidx, h_idx]
        s = jnp.einsum('qhd,qhsjd->qhsj', q_c, kg, preferred_element_type=jnp.float32) * scale
        mask = valid_c[..., None] & (kpos <= qpos_c[:, None, None, None])
        s = jnp.where(mask, s, -jnp.inf)
        p = jax.nn.softmax(s.reshape(qb, h, -1), axis=-1).reshape(s.shape)
        return jnp.einsum('qhsj,qhsjd->qhd', p.astype(v.dtype), vg)

    out = lax.map(attend, xs)
    return out.reshape(b, t, h * hd)


def spatial_gating(u, v, w_s, b_s):
    b, t, _ = u.shape
    c = min(GMLP_CHUNK, t)
    nc = t // c
    dg = D_GMLP // GMLP_GROUPS
    causal = jnp.tril(jnp.ones((c, c), bool))
    w = jnp.where(causal, w_s[:, :c, :c], 0)
    vc = v.reshape(b, nc, c, GMLP_GROUPS, dg)
    mixed = jnp.einsum('gts,bnsgd->bntgd', w, vc) + b_s[:, :c].T[None, None, :, :, None]
    return u * mixed.reshape(b, t, D_GMLP)


def multiscale_pool(xc, prefix, pos0, w_pool, pool_scale):
    b, t, _ = xc.shape
    dg = D_POOL // POOL_GROUPS
    xp = jnp.concatenate([prefix.astype(xc.dtype), xc], axis=1)
    cs = jnp.cumsum(xp.astype(jnp.float32), axis=1)
    cs = jnp.concatenate([jnp.zeros((b, 1, D_POOL), jnp.float32), cs], axis=1)
    cs = cs.reshape(b, POOL_BUF + t + 1, POOL_GROUPS, dg)
    win = jnp.array(POOL_WINDOWS, dtype=jnp.int32)
    ti = jnp.arange(t, dtype=jnp.int32)
    end = POOL_BUF + 1 + ti
    start = end[:, None] - win[None, :]
    g_idx = jnp.arange(POOL_GROUPS, dtype=jnp.int32)[None, :]
    wsum = cs[:, end] - cs[:, start, g_idx]
    count = jnp.minimum(win[None, :], pos0 + ti[:, None] + 1).astype(jnp.float32)
    pooled = wsum / count[None, :, :, None] - xc.reshape(b, t, POOL_GROUPS, dg).astype(jnp.float32)
    y = jnp.einsum('btgd,gde->btge', pooled.astype(xc.dtype), w_pool).reshape(b, t, D_POOL)
    return y * pool_scale, xp[:, -POOL_BUF:]


def trunk_layer(h, p, k_past, v_past, pool_prefix, pos0):
    b, t, _ = h.shape
    xn = rms_norm(h, p['norm1'])
    z = xn @ p['w_in']
    q, k, v, u, vg, xc, gates = jnp.split(z, SPLITS, axis=-1)
    q = q.reshape(b, t, N_HEADS, HEAD_DIM)
    k = k.reshape(b, t, N_HEADS, HEAD_DIM)
    v = v.reshape(b, t, N_HEADS, HEAD_DIM)
    if k_past is None:
        k_all, v_all = k, v
    else:
        k_all = jnp.concatenate([k_past.astype(k.dtype), k], axis=1)
        v_all = jnp.concatenate([v_past.astype(v.dtype), v], axis=1)
    a_out = moba_attention(q, k_all, v_all, pos0)
    v_norm = layer_norm_gain(jax.nn.gelu(vg), p['gmlp_norm'])
    b_out = spatial_gating(jax.nn.gelu(u), v_norm, p['w_spatial'], p['b_spatial'])
    c_out, new_prefix = multiscale_pool(xc, pool_prefix, pos0, p['w_pool'], p['pool_scale'])
    g_a, g_b, g_c = jnp.split(jax.nn.sigmoid(gates), 3, axis=-1)
    merged = (g_a * (a_out @ p['w_branch_a']) + g_b * (b_out @ p['w_branch_b'])
              + g_c * (c_out @ p['w_branch_c']))
    h = h + merged @ p['w_out']
    hn = rms_norm(h, p['norm2'])
    h = h + jnp.square(jax.nn.relu(hn @ p['w_up'])) @ p['w_down']
    return h, k, v, v_norm, new_prefix


def setup_inputs(seed: int = 0) -> dict:
    key = jax.random.key(seed)
    ks = jax.random.split(key, 24)
    f32 = jnp.float32
    n_pages = PAST_LEN // PAGE_SIZE
    n_phys = (DEC_BATCH * n_pages * 5) // 4
    dgp = D_POOL // POOL_GROUPS

    def nrm(k, shape, scale):
        return jax.random.normal(k, shape, f32) * scale

    perm = jax.random.permutation(ks[5], n_phys)[:DEC_BATCH * n_pages]
    return {
        'x_prompt': nrm(ks[0], (BATCH, SEQ, D_MODEL), 1.0),
        'x_sample': nrm(ks[1], (DEC_BATCH, DEC_SEQ, D_MODEL), 1.0),
        'cache_k': nrm(ks[2], (n_phys, DEPTH, PAGE_SIZE, N_HEADS, HEAD_DIM), 1.0),
        'cache_v': nrm(ks[3], (n_phys, DEPTH, PAGE_SIZE, N_HEADS, HEAD_DIM), 1.0),
        'state_pool': nrm(ks[4], (DEC_BATCH, DEPTH, POOL_BUF, D_POOL), 1.0),
        'page_table': perm.reshape(DEC_BATCH, n_pages).astype(jnp.int32),
        'norm1_g': 1.0 + nrm(ks[6], (DEPTH, D_MODEL), 0.02),
        'w_in': nrm(ks[7], (DEPTH, D_MODEL, N_IN), D_MODEL ** -0.5),
        'gmlp_norm_g': 1.0 + nrm(ks[8], (DEPTH, D_GMLP), 0.02),
        'w_spatial': nrm(ks[9], (DEPTH, GMLP_GROUPS, GMLP_CHUNK, GMLP_CHUNK), 0.5 * GMLP_CHUNK ** -0.5),
        'b_spatial': 1.0 + nrm(ks[10], (DEPTH, GMLP_GROUPS, GMLP_CHUNK), 0.02),
        'w_pool': nrm(ks[11], (DEPTH, POOL_GROUPS, dgp, dgp), dgp ** -0.5),
        'pool_scale': 1.0 + nrm(ks[12], (DEPTH, D_POOL), 0.02),
        'w_branch_a': nrm(ks[13], (DEPTH, D_ATTN, D_MODEL), D_ATTN ** -0.5),
        'w_branch_b': nrm(ks[14], (DEPTH, D_GMLP, D_MODEL), D_GMLP ** -0.5),
        'w_branch_c': nrm(ks[15], (DEPTH, D_POOL, D_MODEL), D_POOL ** -0.5),
        'w_out': nrm(ks[16], (DEPTH, D_MODEL, D_MODEL), D_MODEL ** -0.5),
        'norm2_g': 1.0 + nrm(ks[17], (DEPTH, D_MODEL), 0.02),
        'w_up': nrm(ks[18], (DEPTH, D_MODEL, D_FF), D_MODEL ** -0.5),
        'w_down': nrm(ks[19], (DEPTH, D_FF, D_MODEL), D_FF ** -0.5),
        'final_norm_g': 1.0 + nrm(ks[20], (D_MODEL,), 0.02),
    }


def reference(x_prompt, x_sample, cache_k, cache_v, state_pool, page_table, norm1_g, w_in,
              gmlp_norm_g, w_spatial, b_spatial, w_pool, pool_scale, w_branch_a, w_branch_b,
              w_branch_c, w_out, norm2_g, w_up, w_down, final_norm_g):
    dec_b = x_sample.shape[0]
    past_len = page_table.shape[1] * cache_k.shape[2]
    pool0 = jnp.zeros((x_prompt.shape[0], POOL_BUF, D_POOL), x_prompt.dtype)
    hp, hs = x_prompt, x_sample
    kp_l, vp_l, pp_l, ks_l, vs_l, ps_l, gv_l = [], [], [], [], [], [], []
    for l in range(DEPTH):
        p = {'norm1': norm1_g[l], 'w_in': w_in[l], 'gmlp_norm': gmlp_norm_g[l],
             'w_spatial': w_spatial[l], 'b_spatial': b_spatial[l], 'w_pool': w_pool[l],
             'pool_scale': pool_scale[l], 'w_branch_a': w_branch_a[l],
             'w_branch_b': w_branch_b[l], 'w_branch_c': w_branch_c[l], 'w_out': w_out[l],
             'norm2': norm2_g[l], 'w_up': w_up[l], 'w_down': w_down[l]}
        hp, kp, vp, _, pp = trunk_layer(hp, p, None, None, pool0, 0)
        k_past = cache_k[page_table, l].reshape(dec_b, past_len, N_HEADS, HEAD_DIM)
        v_past = cache_v[page_table, l].reshape(dec_b, past_len, N_HEADS, HEAD_DIM)
        hs, ksm, vsm, gv, psm = trunk_layer(hs, p, k_past, v_past, state_pool[:, l], past_len)
        kp_l.append(kp); vp_l.append(vp); pp_l.append(pp)
        ks_l.append(ksm); vs_l.append(vsm); ps_l.append(psm); gv_l.append(gv)
    y_prompt = rms_norm(hp, final_norm_g)
    y_sample = rms_norm(hs, final_norm_g)
    new_k_prompt = jnp.stack(kp_l, axis=1)
    new_v_prompt = jnp.stack(vp_l, axis=1)
    new_pool_prompt = jnp.stack(pp_l, axis=1)
    new_k_sample = jnp.stack(ks_l, axis=1)
    new_v_sample = jnp.stack(vs_l, axis=1)
    new_pool_sample = jnp.stack(ps_l, axis=1)
    new_gmlp_v_sample = jnp.stack(gv_l, axis=1)
    return (y_prompt, y_sample, new_k_prompt, new_v_prompt, new_pool_prompt,
            new_k_sample, new_v_sample, new_pool_sample, new_gmlp_v_sample)
```

```python
import functools

import jax
import jax.numpy as jnp
from jax import lax
from jax.experimental import pallas as pl
from jax.experimental.pallas import tpu as pltpu

F32 = jnp.float32
BF16 = jnp.bfloat16

D_MODEL = 1024
N_HEADS = 8
HEAD_DIM = 64
D_ATTN = N_HEADS * HEAD_DIM
MOBA_BLOCK = 256
MOBA_TOP_K = 3
GMLP_GROUPS = 4
D_GMLP = 256
GMLP_CHUNK = 128
POOL_GROUPS = 4
D_POOL = 256
POOL_WINDOWS = (2, 4, 8, 16)
POOL_BUF = max(POOL_WINDOWS) - 1
D_FF = 4 * D_MODEL
NORM_EPS = 1e-6
N_IN = 3 * D_ATTN + 2 * D_GMLP + D_POOL + 3 * D_MODEL
OFF_Q, OFF_K, OFF_V = 0, D_ATTN, 2 * D_ATTN
OFF_U = 3 * D_ATTN
OFF_VG = OFF_U + D_GMLP
OFF_XC = OFF_VG + D_GMLP
OFF_GATES = OFF_XC + D_POOL

LANES = 128
SUBLANES = 8
HEADS_PER_TILE = LANES // HEAD_DIM
ATTN_SCALE = HEAD_DIM ** -0.5
MASK_NEG = -(2.0 ** 100)
VMEM_LIMIT = 56 * 1024 * 1024

NT_DIMS = (((1,), (1,)), ((), ()))


def _params(*sem):
    return pltpu.CompilerParams(dimension_semantics=sem, vmem_limit_bytes=VMEM_LIMIT)


def _resident(shape):
    nd = len(shape)
    return pl.BlockSpec(shape, lambda *_: (0,) * nd, pipeline_mode=pl.Buffered(1))


def _rms(x, g):
    return x * lax.rsqrt(jnp.mean(x * x, axis=-1, keepdims=True) + NORM_EPS) * g


def _proj_kernel(h_ref, g_ref, w_ref, wkv_t_ref, gn_ref, q_ref, k_ref, v_ref, ug_ref, vn_ref, xc_ref,
                 gates_ref, *, kv_transposed):
    xb = _rms(h_ref[...], g_ref[...]).astype(BF16)

    def seg(lo, width):
        return jnp.dot(xb, w_ref[:, lo:lo + width], preferred_element_type=F32)

    q_ref[...] = seg(OFF_Q, D_ATTN)
    if kv_transposed:
        k_ref[0] = lax.dot_general(wkv_t_ref[0:D_ATTN, :], xb, NT_DIMS, preferred_element_type=F32)
        v_ref[0] = lax.dot_general(wkv_t_ref[D_ATTN:2 * D_ATTN, :], xb, NT_DIMS,
                                   preferred_element_type=F32)
    else:
        k_ref[...] = seg(OFF_K, D_ATTN)
        v_ref[...] = seg(OFF_V, D_ATTN)
    ug_ref[...] = jax.nn.gelu(seg(OFF_U, D_GMLP))
    gv = jax.nn.gelu(seg(OFF_VG, D_GMLP))
    gc = gv - jnp.mean(gv, axis=-1, keepdims=True)
    vn_ref[...] = gc * lax.rsqrt(jnp.mean(gc * gc, axis=-1, keepdims=True) + NORM_EPS) * gn_ref[...]
    xc_ref[...] = seg(OFF_XC, D_POOL)
    for j in range(3):
        z = seg(OFF_GATES + j * D_MODEL, D_MODEL)
        gates_ref[:, j * D_MODEL:(j + 1) * D_MODEL] = jax.nn.sigmoid(z).astype(BF16)


def _proj(h, g, w_in, wkv_t, gn, nseq, tm, kv_transposed):
    n = h.shape[0]
    t = n // nseq
    nt = t // tm
    row = lambda w: pl.BlockSpec((tm, w), lambda bi, ti: (bi * nt + ti, 0))
    flat = lambda w, dt=F32: jax.ShapeDtypeStruct((n, w), dt)
    if kv_transposed:
        kv_spec = pl.BlockSpec((1, D_ATTN, tm), lambda bi, ti: (bi, 0, ti))
        kv_shape = jax.ShapeDtypeStruct((nseq, D_ATTN, t), F32)
    else:
        kv_spec, kv_shape = row(D_ATTN), flat(D_ATTN)
    return pl.pallas_call(
        functools.partial(_proj_kernel, kv_transposed=kv_transposed),
        grid=(nseq, nt),
        in_specs=[row(D_MODEL), _resident((1, D_MODEL)), _resident((D_MODEL, N_IN)),
                  _resident((2 * D_ATTN, D_MODEL)), _resident((1, D_GMLP))],
        out_specs=[row(D_ATTN), kv_spec, kv_spec, row(D_GMLP), row(D_GMLP), row(D_POOL),
                   row(3 * D_MODEL)],
        out_shape=[flat(D_ATTN), kv_shape, kv_shape, flat(D_GMLP), flat(D_GMLP), flat(D_POOL),
                   flat(3 * D_MODEL, BF16)],
        compiler_params=_params("parallel", "parallel"),
    )(h, g, w_in, wkv_t, gn)


def _attn_prompt_kernel(q_ref, kt_ref, vt_ref, o_ref, kaug_ref, vb_ref, kmean_ref, *, k_top):
    i = pl.program_id(2)
    t = kt_ref.shape[2]
    nb = t // MOBA_BLOCK
    nbp = kmean_ref.shape[0]

    @pl.when(i == 0)
    def _():
        kt = kt_ref[0]
        blk_row = lax.broadcasted_iota(jnp.int32, (LANES, t), 0)
        blk_pos = lax.broadcasted_iota(jnp.int32, (LANES, t), 1) // MOBA_BLOCK
        kaug_ref[0:LANES, :] = kt.astype(BF16)
        kaug_ref[LANES:2 * LANES, :] = jnp.where(blk_row == blk_pos, 1.0, 0.0).astype(BF16)
        vb_ref[...] = vt_ref[0].astype(BF16)
        lane = lax.broadcasted_iota(jnp.int32, (LANES, LANES), 1)
        km_t = jnp.zeros((LANES, LANES), F32)
        for n in range(nb):
            col = jnp.sum(kt[:, n * MOBA_BLOCK:(n + 1) * MOBA_BLOCK], axis=1, keepdims=True)
            km_t = jnp.where(lane == n, col * (1.0 / MOBA_BLOCK), km_t)
        kmean_ref[...] = km_t.T[0:nbp, :]

    qf = q_ref[0]
    lane = lax.broadcasted_iota(jnp.int32, (MOBA_BLOCK, LANES), 1)
    blk = lax.broadcasted_iota(jnp.int32, (nbp, MOBA_BLOCK), 0)
    past = blk < i
    rr = lax.broadcasted_iota(jnp.int32, (MOBA_BLOCK, MOBA_BLOCK), 0)
    cc = lax.broadcasted_iota(jnp.int32, (MOBA_BLOCK, MOBA_BLOCK), 1)
    eye = jnp.where(rr == cc, 1.0, 0.0).astype(BF16)
    qpos = i * MOBA_BLOCK + lax.broadcasted_iota(jnp.int32, (MOBA_BLOCK, t), 0)
    causal = lax.broadcasted_iota(jnp.int32, (MOBA_BLOCK, t), 1) <= qpos
    kmean = kmean_ref[...]

    outs = []
    for hh in range(HEADS_PER_TILE):
        qh = jnp.where(lane // HEAD_DIM == hh, qf, 0.0)
        g_t = lax.dot_general(kmean, qh, NT_DIMS, precision=lax.Precision.HIGHEST,
                              preferred_element_type=F32)
        gm = jnp.where(past, g_t, -jnp.inf)
        rank = jnp.zeros((nbp, MOBA_BLOCK), jnp.int32)
        for m in range(nb):
            gr = gm[m:m + 1, :]
            beats = (gr > gm) | ((gr == gm) & (m < blk))
            rank = rank + beats.astype(jnp.int32)
        sel = (past & (rank < k_top) & (gm > -jnp.inf) & (gm < jnp.inf)) | (blk == i)
        sel_t = jnp.where(sel, 1.0, 0.0)
        sel_t = jnp.concatenate([sel_t, jnp.zeros((LANES - nbp, MOBA_BLOCK), F32)], axis=0)
        sel_n = lax.dot_general(eye, sel_t.astype(BF16), NT_DIMS, preferred_element_type=F32)
        bias = jnp.where(sel_n > 0.5, 0.0, MASK_NEG).astype(BF16)
        qs = (qh * ATTN_SCALE).astype(BF16)
        q_aug = jnp.concatenate([qs, bias], axis=1)
        s = jnp.dot(q_aug, kaug_ref[...], preferred_element_type=F32)
        s = jnp.where(causal, s, MASK_NEG)
        mx = jnp.max(s, axis=-1, keepdims=True)
        p = jnp.exp(s - mx)
        den = jnp.sum(p, axis=-1, keepdims=True)
        o = lax.dot_general(p.astype(BF16), vb_ref[...], NT_DIMS, preferred_element_type=F32)
        outs.append(o / den)
    o_ref[0] = jnp.where(lane < HEAD_DIM, outs[0], outs[1])


def _attn_prompt(q, kt, vt):
    b, t, _ = q.shape
    assert t % MOBA_BLOCK == 0
    nb = t // MOBA_BLOCK
    nbp = -(-nb // SUBLANES) * SUBLANES
    assert nbp <= LANES
    qspec = pl.BlockSpec((1, MOBA_BLOCK, LANES), lambda bi, p, i: (bi, i, p))
    kspec = pl.BlockSpec((1, LANES, t), lambda bi, p, i: (bi, p, 0))
    return pl.pallas_call(
        functools.partial(_attn_prompt_kernel, k_top=min(MOBA_TOP_K, nb)),
        grid=(b, D_ATTN // LANES, nb),
        in_specs=[qspec, kspec, kspec],
        out_specs=qspec,
        out_shape=jax.ShapeDtypeStruct((b, t, D_ATTN), F32),
        scratch_shapes=[pltpu.VMEM((2 * LANES, t), BF16), pltpu.VMEM((LANES, t), BF16),
                        pltpu.VMEM((nbp, LANES), F32)],
        compiler_params=_params("parallel", "parallel", "arbitrary"),
    )(q, kt, vt)


def _attn_sample_kernel(pt_ref, q_ref, kn_ref, vn_ref, *refs, n_pages, k_top):
    del pt_ref
    kp = refs[:n_pages]
    vp = refs[n_pages:2 * n_pages]
    o_ref = refs[2 * n_pages]
    ts = q_ref.shape[1]
    page = kp[0].shape[3]
    ppb = MOBA_BLOCK // page
    nb = n_pages // ppb
    rows = N_HEADS * ts

    q = q_ref[0]
    lane_head = lax.broadcasted_iota(jnp.int32, (ts, D_ATTN), 1) // HEAD_DIM
    q_bd = jnp.concatenate([jnp.where(lane_head == h, q, 0.0) for h in range(N_HEADS)], axis=0)
    qs = (q_bd * ATTN_SCALE).astype(BF16)

    lane = lax.broadcasted_iota(jnp.int32, (D_ATTN, LANES), 1)
    km_t = jnp.zeros((D_ATTN, LANES), F32)
    s_list = []
    for n in range(nb):
        kt = jnp.concatenate([kp[n * ppb + j][0, 0] for j in range(ppb)], axis=1)
        col = jnp.sum(kt, axis=1, keepdims=True) * (1.0 / MOBA_BLOCK)
        km_t = jnp.where(lane == n, col, km_t)
        s_list.append(jnp.dot(qs, kt.astype(BF16), preferred_element_type=F32))

    gate = jnp.dot(q_bd, km_t, precision=lax.Precision.HIGHEST, preferred_element_type=F32)
    blk = lax.broadcasted_iota(jnp.int32, (rows, LANES), 1)
    gm = jnp.where(blk < nb, gate, -jnp.inf)
    rank = jnp.zeros((rows, LANES), jnp.int32)
    for m in range(nb):
        gc = gm[:, m:m + 1]
        beats = (gc > gm) | ((gc == gm) & (m < blk))
        rank = rank + beats.astype(jnp.int32)
    sel = (rank < k_top) & (gm > -jnp.inf) & (gm < jnp.inf)
    bias = jnp.where(sel, 0.0, MASK_NEG)

    pad = jnp.zeros((LANES - ts, D_ATTN), F32)
    k_new = jnp.concatenate([kn_ref[0], pad], axis=0).astype(BF16)
    v_new = jnp.concatenate([vn_ref[0], pad], axis=0).astype(BF16)
    s_new = lax.dot_general(qs, k_new, NT_DIMS, preferred_element_type=F32)
    qi = lax.broadcasted_iota(jnp.int32, (rows, LANES), 0) % ts
    s_new = jnp.where(blk <= qi, s_new, MASK_NEG)

    mx = jnp.max(s_new, axis=-1, keepdims=True)
    for n in range(nb):
        s_list[n] = s_list[n] + bias[:, n:n + 1]
        mx = jnp.maximum(mx, jnp.max(s_list[n], axis=-1, keepdims=True))
    p_new = jnp.exp(s_new - mx)
    den = jnp.sum(p_new, axis=-1, keepdims=True)
    acc = jnp.dot(p_new.astype(BF16), v_new, preferred_element_type=F32)
    for n in range(nb):
        p_n = jnp.exp(s_list[n] - mx)
        den = den + jnp.sum(p_n, axis=-1, keepdims=True)
        vt = jnp.concatenate([vp[n * ppb + j][0, 0] for j in range(ppb)], axis=1).astype(BF16)
        acc = acc + lax.dot_general(p_n.astype(BF16), vt, NT_DIMS, preferred_element_type=F32)
    acc = acc / den
    out = jnp.zeros((ts, D_ATTN), F32)
    for h in range(N_HEADS):
        out = out + jnp.where(lane_head == h, acc[h * ts:(h + 1) * ts, :], 0.0)
    o_ref[0] = out


def _attn_sample(q, k_new, v_new, cache_kt, cache_vt, page_table, layer):
    b, ts, _ = q.shape
    n_pages = page_table.shape[1]
    page = cache_kt.shape[3]
    past_len = n_pages * page
    assert ts == SUBLANES and MOBA_BLOCK % page == 0 and past_len % MOBA_BLOCK == 0
    nb = past_len // MOBA_BLOCK
    assert nb <= LANES
    tok = pl.BlockSpec((1, ts, D_ATTN), lambda bi, pt: (bi, 0, 0))

    def page_spec(j):
        return pl.BlockSpec((1, 1, D_ATTN, page), lambda bi, pt: (pt[bi, j], layer, 0, 0))

    pages = [page_spec(j) for j in range(n_pages)]
    return pl.pallas_call(
        functools.partial(_attn_sample_kernel, n_pages=n_pages, k_top=min(MOBA_TOP_K, nb)),
        grid_spec=pltpu.PrefetchScalarGridSpec(
            num_scalar_prefetch=1,
            grid=(b,),
            in_specs=[tok, tok, tok] + pages + pages,
            out_specs=tok,
        ),
        out_shape=jax.ShapeDtypeStruct((b, ts, D_ATTN), F32),
        compiler_params=_params("parallel"),
    )(page_table, q, k_new, v_new, *([cache_kt] * n_pages), *([cache_vt] * n_pages))


def _lane_windows(shape):
    grp = lax.broadcasted_iota(jnp.int32, shape, len(shape) - 1) // (D_POOL // POOL_GROUPS)
    win = jnp.full(shape, POOL_WINDOWS[0], jnp.int32)
    for gi in range(1, POOL_GROUPS):
        win = jnp.where(grp == gi, POOL_WINDOWS[gi], win)
    return win


def _branch_prompt_kernel(ug_ref, vn_ref, xc_ref, prev_ref, ws_ref, bs_ref, wp_ref, ps_ref,
                          b_ref, c_ref):
    ti = pl.program_id(1)
    tm = ug_ref.shape[1]
    dg = D_GMLP // GMLP_GROUPS

    wr = lax.broadcasted_iota(jnp.int32, (GMLP_CHUNK, GMLP_GROUPS * GMLP_CHUNK), 0)
    wc = lax.broadcasted_iota(jnp.int32, (GMLP_CHUNK, GMLP_GROUPS * GMLP_CHUNK), 1) % GMLP_CHUNK
    w_tril = jnp.where(wc <= wr, ws_ref[...], 0.0).astype(BF16)
    lane_grp = lax.broadcasted_iota(jnp.int32, (GMLP_CHUNK, D_GMLP), 1) // dg
    for c in range(tm // GMLP_CHUNK):
        rows = slice(c * GMLP_CHUNK, (c + 1) * GMLP_CHUNK)
        vc = vn_ref[0, rows, :]
        v_stack = jnp.concatenate([jnp.where(lane_grp == gi, vc, 0.0) for gi in range(GMLP_GROUPS)],
                                  axis=0).astype(BF16)
        mixed = jnp.dot(w_tril, v_stack, preferred_element_type=F32) + bs_ref[...]
        b_ref[0, rows, :] = ug_ref[0, rows, :] * mixed

    x = xc_ref[0]
    prev = jnp.where(ti > 0, prev_ref[0], 0.0)
    xe = jnp.concatenate([prev, x], axis=0)
    s2 = xe + pltpu.roll(xe, 1, axis=0)
    s4 = s2 + pltpu.roll(s2, 2, axis=0)
    s8 = s4 + pltpu.roll(s4, 4, axis=0)
    s16 = s8 + pltpu.roll(s8, 8, axis=0)
    win = _lane_windows((tm, D_POOL))
    pre = 2 * SUBLANES
    wsum = jnp.where(win == 2, s2[pre:], jnp.where(win == 4, s4[pre:],
                     jnp.where(win == 8, s8[pre:], s16[pre:])))
    pos = ti * tm + lax.broadcasted_iota(jnp.int32, (tm, D_POOL), 0)
    count = jnp.minimum(win, pos + 1).astype(F32)
    pooled = wsum / count - x
    y = jnp.dot(pooled.astype(BF16), wp_ref[...], preferred_element_type=F32)
    c_ref[0] = y * ps_ref[...]


def _branch_prompt(ug, vn, xc, ws_cat, bs_t, wp_bd, ps, tm):
    b, t, _ = ug.shape
    assert t % tm == 0 and tm % GMLP_CHUNK == 0
    pre = 2 * SUBLANES
    tile = pl.BlockSpec((1, tm, D_GMLP), lambda bi, ti: (bi, ti, 0))
    prev = pl.BlockSpec((1, pre, D_POOL), lambda bi, ti: (bi, jnp.maximum(ti * (tm // pre) - 1, 0), 0))
    return pl.pallas_call(
        _branch_prompt_kernel,
        grid=(b, t // tm),
        in_specs=[tile, tile, tile, prev, _resident(ws_cat.shape), _resident(bs_t.shape),
                  _resident(wp_bd.shape), _resident(ps.shape)],
        out_specs=[tile, tile],
        out_shape=[jax.ShapeDtypeStruct((b, t, D_GMLP), F32), jax.ShapeDtypeStruct((b, t, D_POOL), F32)],
        compiler_params=_params("parallel", "parallel"),
    )(ug, vn, xc, xc, ws_cat, bs_t, wp_bd, ps)


def _branch_sample_kernel(ug_ref, vn_ref, xc_ref, pre_ref, wc_ref, bs_ref, wp_ref, ps_ref,
                          b_ref, c_ref, *, pos0):
    nb, ts, _ = ug_ref.shape
    vn = vn_ref[...]
    t_idx = lax.broadcasted_iota(jnp.int32, (ts, D_GMLP), 0)
    mixed = jnp.zeros((nb, ts, D_GMLP), F32) + bs_ref[...][None]
    for s in range(ts):
        coef = jnp.where(s <= t_idx, wc_ref[s], 0.0)
        mixed = mixed + coef[None] * vn[:, s:s + 1, :]
    b_ref[...] = ug_ref[...] * mixed

    x = xc_ref[...]
    pre = pre_ref[...]
    win = _lane_windows((ts, D_POOL))
    end = POOL_BUF + t_idx
    wsum = jnp.zeros((nb, ts, D_POOL), F32)
    for r in range(POOL_BUF + ts):
        row = pre[:, r:r + 1, :] if r < POOL_BUF else x[:, r - POOL_BUF:r - POOL_BUF + 1, :]
        inside = (r <= end) & (r > end - win)
        wsum = wsum + jnp.where(inside, 1.0, 0.0)[None] * row
    count = jnp.minimum(win, pos0 + t_idx + 1).astype(F32)
    pooled = wsum / count[None] - x
    y = jnp.dot(pooled.reshape(nb * ts, D_POOL).astype(BF16), wp_ref[...], preferred_element_type=F32)
    c_ref[...] = (y * ps_ref[...]).reshape(nb, ts, D_POOL)


def _branch_sample(ug, vn, xc, prefix, wc, bs_t, wp_bd, ps, pos0):
    b, ts, _ = ug.shape
    assert ts == SUBLANES
    full = lambda shape: pl.BlockSpec(shape, lambda i: (0,) * len(shape))
    return pl.pallas_call(
        functools.partial(_branch_sample_kernel, pos0=pos0),
        grid=(1,),
        in_specs=[full(ug.shape), full(vn.shape), full(xc.shape), full(prefix.shape), full(wc.shape),
                  full(bs_t.shape), full(wp_bd.shape), full(ps.shape)],
        out_specs=[full(ug.shape), full(xc.shape)],
        out_shape=[jax.ShapeDtypeStruct(ug.shape, F32), jax.ShapeDtypeStruct(xc.shape, F32)],
        compiler_params=_params("arbitrary"),
    )(ug, vn, xc, prefix, wc, bs_t, wp_bd, ps)


def _merge_kernel(h_ref, a_ref, b_ref, c_ref, gates_ref, wa_ref, wb_ref, wc_ref, wo_ref, o_ref):
    def branch(x_ref, w_ref, j):
        y = jnp.dot(x_ref[...].astype(BF16), w_ref[...], preferred_element_type=F32)
        return gates_ref[:, j * D_MODEL:(j + 1) * D_MODEL].astype(F32) * y

    merged = branch(a_ref, wa_ref, 0) + branch(b_ref, wb_ref, 1) + branch(c_ref, wc_ref, 2)
    o_ref[...] = h_ref[...] + jnp.dot(merged.astype(BF16), wo_ref[...], preferred_element_type=F32)


def _merge(h, a, b, c, gates, wa, wb, wc, wo, tm):
    t = h.shape[0]
    row = lambda w: pl.BlockSpec((tm, w), lambda i: (i, 0))
    return pl.pallas_call(
        _merge_kernel,
        grid=(t // tm,),
        in_specs=[row(D_MODEL), row(D_ATTN), row(D_GMLP), row(D_POOL), row(3 * D_MODEL),
                  _resident(wa.shape), _resident(wb.shape), _resident(wc.shape), _resident(wo.shape)],
        out_specs=row(D_MODEL),
        out_shape=jax.ShapeDtypeStruct((t, D_MODEL), F32),
        compiler_params=_params("parallel"),
    )(h, a, b, c, gates, wa, wb, wc, wo)


def _mlp_kernel(h_ref, g_ref, wu_ref, wd_ref, gf_ref, o_ref, *, final, ff_chunk):
    h = h_ref[...]
    xb = _rms(h, g_ref[...]).astype(BF16)
    acc = h
    for c in range(D_FF // ff_chunk):
        cols = slice(c * ff_chunk, (c + 1) * ff_chunk)
        up = jnp.dot(xb, wu_ref[:, cols], preferred_element_type=F32)
        act = jnp.square(jnp.maximum(up, 0.0)).astype(BF16)
        acc = acc + jnp.dot(act, wd_ref[cols, :], preferred_element_type=F32)
    o_ref[...] = _rms(acc, gf_ref[...]) if final else acc


def _mlp(h, g, wu, wd, gf, final, tm):
    t = h.shape[0]
    row = pl.BlockSpec((tm, D_MODEL), lambda i: (i, 0))
    return pl.pallas_call(
        functools.partial(_mlp_kernel, final=final, ff_chunk=1024),
        grid=(t // tm,),
        in_specs=[row, _resident((1, D_MODEL)), _resident(wu.shape), _resident(wd.shape),
                  _resident((1, D_MODEL))],
        out_specs=row,
        out_shape=jax.ShapeDtypeStruct((t, D_MODEL), F32),
        compiler_params=_params("parallel"),
    )(h, g, wu, wd, gf)


def _block_diag(w):
    g, d, e = w.shape
    eye = jnp.eye(g, dtype=w.dtype)
    return (eye[:, None, :, None] * w[:, :, None, :]).reshape(g * d, g * e)


def _pages_channel_major(cache):
    n_phys, depth, page, _, _ = cache.shape
    return jnp.transpose(cache, (0, 1, 3, 4, 2)).reshape(n_phys, depth, D_ATTN, page)


def _heads_last(kt):
    b, depth, _, t = kt.shape
    return jnp.transpose(kt.reshape(b, depth, N_HEADS, HEAD_DIM, t), (0, 1, 4, 2, 3))


def kernel(x_prompt, x_sample, cache_k, cache_v, state_pool, page_table, norm1_g, w_in, gmlp_norm_g,
           w_spatial, b_spatial, w_pool, pool_scale, w_branch_a, w_branch_b, w_branch_c, w_out,
           norm2_g, w_up, w_down, final_norm_g):
    bp, tp, _ = x_prompt.shape
    bs, ts, _ = x_sample.shape
    depth = w_in.shape[0]
    past_len = page_table.shape[1] * cache_k.shape[2]
    dg = D_GMLP // GMLP_GROUPS
    tm = min(512, tp)
    tms = min(512, bs * ts)

    ckt = _pages_channel_major(cache_k)
    cvt = _pages_channel_major(cache_v)
    w_in_b = w_in.astype(BF16)
    wkv_t = jnp.transpose(w_in_b[:, :, OFF_K:OFF_K + 2 * D_ATTN], (0, 2, 1))
    wa_b, wb_b, wc_b = w_branch_a.astype(BF16), w_branch_b.astype(BF16), w_branch_c.astype(BF16)
    wo_b, wu_b, wd_b = w_out.astype(BF16), w_up.astype(BF16), w_down.astype(BF16)
    gf = final_norm_g.reshape(1, D_MODEL)

    hp = x_prompt.reshape(bp * tp, D_MODEL)
    hs = x_sample.reshape(bs * ts, D_MODEL)
    outs = {n: [] for n in ("kp", "vp", "pp", "ks", "vs", "ps", "gv")}
    for l in range(depth):
        g1 = norm1_g[l].reshape(1, D_MODEL)
        g2 = norm2_g[l].reshape(1, D_MODEL)
        gn = gmlp_norm_g[l].reshape(1, D_GMLP)
        ps = pool_scale[l].reshape(1, D_POOL)
        wp_bd = _block_diag(w_pool[l]).astype(BF16)
        ws_cat = jnp.transpose(w_spatial[l], (1, 0, 2)).reshape(GMLP_CHUNK, GMLP_GROUPS * GMLP_CHUNK)
        bs_full = jnp.repeat(b_spatial[l].T, dg, axis=1)
        wc_s = jnp.repeat(jnp.transpose(w_spatial[l][:, :ts, :ts], (2, 1, 0)), dg, axis=2)
        bs_s = bs_full[:ts]

        q, kt, vt, ug, vn, xc, gates = _proj(hp, g1, w_in_b[l], wkv_t[l], gn, bp, tm, True)
        r3 = lambda a: a.reshape(bp, tp, a.shape[-1])
        a_out = _attn_prompt(r3(q), kt, vt)
        b_out, c_out = _branch_prompt(r3(ug), r3(vn), r3(xc), ws_cat, bs_full, wp_bd, ps, tm)
        h1 = _merge(hp, a_out.reshape(bp * tp, D_ATTN), b_out.reshape(bp * tp, D_GMLP),
                    c_out.reshape(bp * tp, D_POOL), gates, wa_b[l], wb_b[l], wc_b[l], wo_b[l], tm)
        hp = _mlp(h1, g2, wu_b[l], wd_b[l], gf, l == depth - 1, tm)
        outs["kp"].append(kt)
        outs["vp"].append(vt)
        outs["pp"].append(r3(xc)[:, tp - POOL_BUF:])

        q, k, v, ug, vn, xc, gates = _proj(hs, g1, w_in_b[l], wkv_t[l], gn, 1, tms, False)
        s3 = lambda a: a.reshape(bs, ts, a.shape[-1])
        a_out = _attn_sample(s3(q), s3(k), s3(v), ckt, cvt, page_table, l)
        prefix = state_pool[:, l]
        b_out, c_out = _branch_sample(s3(ug), s3(vn), s3(xc), prefix, wc_s, bs_s, wp_bd, ps, past_len)
        h1 = _merge(hs, a_out.reshape(bs * ts, D_ATTN), b_out.reshape(bs * ts, D_GMLP),
                    c_out.reshape(bs * ts, D_POOL), gates, wa_b[l], wb_b[l], wc_b[l], wo_b[l], tms)
        hs = _mlp(h1, g2, wu_b[l], wd_b[l], gf, l == depth - 1, tms)
        outs["ks"].append(k.reshape(bs, ts, N_HEADS, HEAD_DIM))
        outs["vs"].append(v.reshape(bs, ts, N_HEADS, HEAD_DIM))
        outs["ps"].append(jnp.concatenate([prefix, s3(xc)], axis=1)[:, -POOL_BUF:])
        outs["gv"].append(s3(vn))

    stack = lambda n: jnp.stack(outs[n], axis=1)
    return (hp.reshape(bp, tp, D_MODEL), hs.reshape(bs, ts, D_MODEL), _heads_last(stack("kp")),
            _heads_last(stack("vp")), stack("pp"), stack("ks"), stack("vs"), stack("ps"), stack("gv"))
```

```python
import functools

import jax
import jax.numpy as jnp
from jax import lax
from jax.experimental import pallas as pl
from jax.experimental.pallas import tpu as pltpu

F32 = jnp.float32
BF16 = jnp.bfloat16

D_MODEL = 1024
N_HEADS = 8
HEAD_DIM = 64
D_ATTN = N_HEADS * HEAD_DIM
MOBA_BLOCK = 256
MOBA_TOP_K = 3
GMLP_GROUPS = 4
D_GMLP = 256
GMLP_CHUNK = 128
POOL_GROUPS = 4
D_POOL = 256
POOL_WINDOWS = (2, 4, 8, 16)
POOL_BUF = max(POOL_WINDOWS) - 1
D_FF = 4 * D_MODEL
NORM_EPS = 1e-6
N_IN = 3 * D_ATTN + 2 * D_GMLP + D_POOL + 3 * D_MODEL
OFF_Q, OFF_K, OFF_V = 0, D_ATTN, 2 * D_ATTN
OFF_U = 3 * D_ATTN
OFF_VG = OFF_U + D_GMLP
OFF_XC = OFF_VG + D_GMLP
OFF_GATES = OFF_XC + D_POOL

LANES = 128
SUBLANES = 8
HEADS_PER_TILE = LANES // HEAD_DIM
ATTN_SCALE = HEAD_DIM ** -0.5
LOG2E = 1.4426950408889634
MASK_NEG = -(2.0 ** 100)
VMEM_LIMIT = 56 * 1024 * 1024

NT_DIMS = (((1,), (1,)), ((), ()))


def _params(*sem):
    return pltpu.CompilerParams(dimension_semantics=sem, vmem_limit_bytes=VMEM_LIMIT)


def _resident(shape):
    nd = len(shape)
    return pl.BlockSpec(shape, lambda *_: (0,) * nd, pipeline_mode=pl.Buffered(1))


def _layer_resident(stacked_shape, layer):
    nd = len(stacked_shape) - 1
    return pl.BlockSpec((None,) + tuple(stacked_shape[1:]), lambda *_: (layer,) + (0,) * nd,
                        pipeline_mode=pl.Buffered(1))


def _rms(x, g):
    return x * lax.rsqrt(jnp.mean(x * x, axis=-1, keepdims=True) + NORM_EPS) * g


def _proj_kernel(h_ref, g_ref, w_ref, wkv_t_ref, gn_ref, q_ref, k_ref, v_ref, ug_ref, vn_ref, xc_ref,
                 gates_ref, *, kv_transposed):
    xb = _rms(h_ref[...], g_ref[...]).astype(BF16)

    def seg(lo, width):
        return jnp.dot(xb, w_ref[:, lo:lo + width], preferred_element_type=F32)

    q_ref[...] = seg(OFF_Q, D_ATTN)
    if kv_transposed:
        k_ref[0] = lax.dot_general(wkv_t_ref[0:D_ATTN, :], xb, NT_DIMS, preferred_element_type=F32)
        v_ref[0] = lax.dot_general(wkv_t_ref[D_ATTN:2 * D_ATTN, :], xb, NT_DIMS,
                                   preferred_element_type=F32)
    else:
        k_ref[...] = seg(OFF_K, D_ATTN)
        v_ref[...] = seg(OFF_V, D_ATTN)
    ug_ref[...] = jax.nn.gelu(seg(OFF_U, D_GMLP))
    gv = jax.nn.gelu(seg(OFF_VG, D_GMLP))
    gc = gv - jnp.mean(gv, axis=-1, keepdims=True)
    vn_ref[...] = gc * lax.rsqrt(jnp.mean(gc * gc, axis=-1, keepdims=True) + NORM_EPS) * gn_ref[...]
    xc_ref[...] = seg(OFF_XC, D_POOL)
    for j in range(3):
        z = seg(OFF_GATES + j * D_MODEL, D_MODEL)
        gates_ref[:, j * D_MODEL:(j + 1) * D_MODEL] = jax.nn.sigmoid(z).astype(BF16)


def _proj(h, g, w_in, wkv_t, gn, layer, nseq, tm, kv_transposed):
    n = h.shape[0]
    t = n // nseq
    nt = t // tm
    row = lambda w: pl.BlockSpec((tm, w), lambda bi, ti: (bi * nt + ti, 0))
    flat = lambda w, dt=F32: jax.ShapeDtypeStruct((n, w), dt)
    if kv_transposed:
        kv_spec = pl.BlockSpec((1, D_ATTN, tm), lambda bi, ti: (bi, 0, ti))
        kv_shape = jax.ShapeDtypeStruct((nseq, D_ATTN, t), F32)
    else:
        kv_spec, kv_shape = row(D_ATTN), flat(D_ATTN)
    return pl.pallas_call(
        functools.partial(_proj_kernel, kv_transposed=kv_transposed),
        grid=(nseq, nt),
        in_specs=[row(D_MODEL), _resident((1, D_MODEL)), _layer_resident(w_in.shape, layer),
                  _layer_resident(wkv_t.shape, layer), _resident((1, D_GMLP))],
        out_specs=[row(D_ATTN), kv_spec, kv_spec, row(D_GMLP), row(D_GMLP), row(D_POOL),
                   row(3 * D_MODEL)],
        out_shape=[flat(D_ATTN), kv_shape, kv_shape, flat(D_GMLP), flat(D_GMLP), flat(D_POOL),
                   flat(3 * D_MODEL, BF16)],
        compiler_params=_params("parallel", "parallel"),
    )(h, g, w_in, wkv_t, gn)


def _attn_prompt_kernel(q_ref, kt_ref, vt_ref, o_ref, kaug_ref, vb_ref, kbd_ref, qaug_ref, s_ref,
                        mx_ref, l_ref, acc_ref, *, k_top):
    i = pl.program_id(1)
    nb, n_pairs = kaug_ref.shape[0], kaug_ref.shape[1]
    nbp = SUBLANES
    fold = lambda x: (x[:, 0:LANES], x[:, LANES:2 * LANES])

    @pl.when(i == 0)
    def _():
        lane = lax.broadcasted_iota(jnp.int32, (LANES, LANES), 1)
        oh_row = lax.broadcasted_iota(jnp.int32, (LANES, MOBA_BLOCK), 0)
        half = lax.broadcasted_iota(jnp.int32, (nbp, LANES), 1) // HEAD_DIM
        kbd_ref[...] = jnp.zeros(kbd_ref.shape, F32)
        for p in range(n_pairs):
            chans = slice(p * LANES, (p + 1) * LANES)
            km_t = jnp.zeros((LANES, LANES), F32)
            for n in range(nb):
                cols = slice(n * MOBA_BLOCK, (n + 1) * MOBA_BLOCK)
                ktn = kt_ref[0, chans, cols]
                kaug_ref[n, p, 0:LANES, :] = ktn.astype(BF16)
                onehot = (oh_row % nbp == n) & (oh_row // (HEADS_PER_TILE * nbp) == p)
                kaug_ref[n, p, LANES:2 * LANES, :] = jnp.where(onehot, 1.0, 0.0).astype(BF16)
                vb_ref[n, p] = vt_ref[0, chans, cols].astype(BF16)
                col = jnp.sum(ktn, axis=1, keepdims=True) * (1.0 / MOBA_BLOCK)
                km_t = jnp.where(lane == n, col, km_t)
            km = km_t.T[0:nbp, :]
            for hh in range(HEADS_PER_TILE):
                h = p * HEADS_PER_TILE + hh
                kbd_ref[h * nbp:(h + 1) * nbp, chans] = jnp.where(half == hh, km, 0.0)

    qf = q_ref[0]
    g = lax.dot_general(kbd_ref[...], qf, NT_DIMS, precision=lax.Precision.HIGHEST,
                        preferred_element_type=F32)
    gm = g.reshape(N_HEADS, nbp, MOBA_BLOCK)
    blk = lax.broadcasted_iota(jnp.int32, gm.shape, 1)
    past = blk < i
    gm = jnp.where(past, gm, -jnp.inf)
    rank = jnp.zeros(gm.shape, jnp.int32)
    for m in range(nb):
        gr = gm[:, m:m + 1, :]
        beats = (gr > gm) | ((gr == gm) & (m < blk))
        rank = rank + beats.astype(jnp.int32)
    sel = past & (rank < k_top) & (gm > -jnp.inf) & (gm < jnp.inf)
    sel_t = jnp.where(sel, 1.0, 0.0).reshape(N_HEADS * nbp, MOBA_BLOCK)
    sel_t = jnp.concatenate([sel_t, jnp.zeros((LANES - N_HEADS * nbp, MOBA_BLOCK), F32)], axis=0)
    rr = lax.broadcasted_iota(jnp.int32, (MOBA_BLOCK, MOBA_BLOCK), 0)
    cc = lax.broadcasted_iota(jnp.int32, (MOBA_BLOCK, MOBA_BLOCK), 1)
    eye = jnp.where(rr == cc, 1.0, 0.0).astype(BF16)
    sel_n = lax.dot_general(eye, sel_t.astype(BF16), NT_DIMS, preferred_element_type=F32)
    bias_all = jnp.where(sel_n > 0.5, 0.0, MASK_NEG)
    lane = lax.broadcasted_iota(jnp.int32, (MOBA_BLOCK, LANES), 1)
    for h in range(N_HEADS):
        p, hh = divmod(h, HEADS_PER_TILE)
        qh = jnp.where(lane // HEAD_DIM == hh, qf[:, p * LANES:(p + 1) * LANES], 0.0)
        qaug_ref[h, :, 0:LANES] = (qh * (ATTN_SCALE * LOG2E)).astype(BF16)
        qaug_ref[h, :, LANES:2 * LANES] = jnp.where(lane // nbp == h, bias_all, 0.0).astype(BF16)

    mx_ref[...] = jnp.full(mx_ref.shape, MASK_NEG, F32)

    def scores(n, carry):
        for h in range(N_HEADS):
            s = jnp.dot(qaug_ref[h], kaug_ref[n, h // HEADS_PER_TILE],
                        preferred_element_type=F32)
            s_ref[h, n] = s
            lo, hi = fold(s)
            mx_ref[h] = jnp.maximum(mx_ref[h], jnp.maximum(lo, hi))
        return carry

    lax.fori_loop(0, i, scores, 0)

    causal = cc <= rr
    for h in range(N_HEADS):
        p = h // HEADS_PER_TILE
        s_own = jnp.dot(qaug_ref[h, :, 0:LANES], kaug_ref[i, p, 0:LANES, :],
                        preferred_element_type=F32)
        s_own = jnp.where(causal, s_own, MASK_NEG)
        lo, hi = fold(s_own)
        mx = jnp.max(jnp.maximum(mx_ref[h], jnp.maximum(lo, hi)), axis=-1, keepdims=True)
        mx_ref[h] = jnp.broadcast_to(mx, (MOBA_BLOCK, LANES))
        p_own = jnp.exp2(s_own - mx)
        lo, hi = fold(p_own)
        l_ref[h] = lo + hi
        acc_ref[h] = lax.dot_general(p_own.astype(BF16), vb_ref[i, p], NT_DIMS,
                                     preferred_element_type=F32)

    def values(n, carry):
        for h in range(N_HEADS):
            mx = mx_ref[h]
            lo, hi = fold(s_ref[h, n])
            p_lo = jnp.exp2(lo - mx)
            p_hi = jnp.exp2(hi - mx)
            l_ref[h] += p_lo + p_hi
            pb = jnp.concatenate([p_lo, p_hi], axis=1).astype(BF16)
            acc_ref[h] += lax.dot_general(pb, vb_ref[n, h // HEADS_PER_TILE], NT_DIMS,
                                          preferred_element_type=F32)
        return carry

    lax.fori_loop(0, i, values, 0)

    for p in range(n_pairs):
        o = [acc_ref[p * HEADS_PER_TILE + hh]
             / jnp.sum(l_ref[p * HEADS_PER_TILE + hh], axis=-1, keepdims=True)
             for hh in range(HEADS_PER_TILE)]
        o_ref[0, :, p * LANES:(p + 1) * LANES] = jnp.where(lane < HEAD_DIM, o[0], o[1])


def _attn_prompt(q, kt, vt):
    b, t, _ = q.shape
    assert t % MOBA_BLOCK == 0
    nb = t // MOBA_BLOCK
    n_pairs = D_ATTN // LANES
    assert nb <= SUBLANES and N_HEADS * SUBLANES <= LANES
    qspec = pl.BlockSpec((1, MOBA_BLOCK, D_ATTN), lambda bi, i: (bi, i, 0))
    kspec = pl.BlockSpec((1, D_ATTN, t), lambda bi, i: (bi, 0, 0))
    head_acc = pltpu.VMEM((N_HEADS, MOBA_BLOCK, LANES), F32)
    return pl.pallas_call(
        functools.partial(_attn_prompt_kernel, k_top=min(MOBA_TOP_K, nb)),
        grid=(b, nb),
        in_specs=[qspec, kspec, kspec],
        out_specs=qspec,
        out_shape=jax.ShapeDtypeStruct((b, t, D_ATTN), F32),
        scratch_shapes=[pltpu.VMEM((nb, n_pairs, 2 * LANES, MOBA_BLOCK), BF16),
                        pltpu.VMEM((nb, n_pairs, LANES, MOBA_BLOCK), BF16),
                        pltpu.VMEM((N_HEADS * SUBLANES, D_ATTN), F32),
                        pltpu.VMEM((N_HEADS, MOBA_BLOCK, 2 * LANES), BF16),
                        pltpu.VMEM((N_HEADS, nb, MOBA_BLOCK, MOBA_BLOCK), F32),
                        head_acc, head_acc, head_acc],
        compiler_params=_params("parallel", "arbitrary"),
    )(q, kt, vt)


def _attn_sample_kernel(pt_ref, q_ref, kn_ref, vn_ref, *refs, n_pages, k_top):
    del pt_ref
    kp = refs[:n_pages]
    vp = refs[n_pages:2 * n_pages]
    o_ref = refs[2 * n_pages]
    ts = q_ref.shape[1]
    page = kp[0].shape[3]
    ppb = MOBA_BLOCK // page
    nb = n_pages // ppb
    rows = N_HEADS * ts

    q = q_ref[0]
    lane_head = lax.broadcasted_iota(jnp.int32, (ts, D_ATTN), 1) // HEAD_DIM
    q_bd = jnp.concatenate([jnp.where(lane_head == h, q, 0.0) for h in range(N_HEADS)], axis=0)
    qs = (q_bd * ATTN_SCALE).astype(BF16)

    lane = lax.broadcasted_iota(jnp.int32, (D_ATTN, LANES), 1)
    km_t = jnp.zeros((D_ATTN, LANES), F32)
    s_list = []
    for n in range(nb):
        kt = jnp.concatenate([kp[n * ppb + j][0, 0] for j in range(ppb)], axis=1)
        col = jnp.sum(kt, axis=1, keepdims=True) * (1.0 / MOBA_BLOCK)
        km_t = jnp.where(lane == n, col, km_t)
        s_list.append(jnp.dot(qs, kt.astype(BF16), preferred_element_type=F32))

    gate = jnp.dot(q_bd, km_t, precision=lax.Precision.HIGHEST, preferred_element_type=F32)
    blk = lax.broadcasted_iota(jnp.int32, (rows, LANES), 1)
    gm = jnp.where(blk < nb, gate, -jnp.inf)
    rank = jnp.zeros((rows, LANES), jnp.int32)
    for m in range(nb):
        gc = gm[:, m:m + 1]
        beats = (gc > gm) | ((gc == gm) & (m < blk))
        rank = rank + beats.astype(jnp.int32)
    sel = (rank < k_top) & (gm > -jnp.inf) & (gm < jnp.inf)
    bias = jnp.where(sel, 0.0, MASK_NEG)

    pad = jnp.zeros((LANES - ts, D_ATTN), F32)
    k_new = jnp.concatenate([kn_ref[0], pad], axis=0).astype(BF16)
    v_new = jnp.concatenate([vn_ref[0], pad], axis=0).astype(BF16)
    s_new = lax.dot_general(qs, k_new, NT_DIMS, preferred_element_type=F32)
    qi = lax.broadcasted_iota(jnp.int32, (rows, LANES), 0) % ts
    s_new = jnp.where(blk <= qi, s_new, MASK_NEG)

    mx = jnp.max(s_new, axis=-1, keepdims=True)
    for n in range(nb):
        s_list[n] = s_list[n] + bias[:, n:n + 1]
        mx = jnp.maximum(mx, jnp.max(s_list[n], axis=-1, keepdims=True))
    p_new = jnp.exp(s_new - mx)
    den = jnp.sum(p_new, axis=-1, keepdims=True)
    acc = jnp.dot(p_new.astype(BF16), v_new, preferred_element_type=F32)
    for n in range(nb):
        p_n = jnp.exp(s_list[n] - mx)
        den = den + jnp.sum(p_n, axis=-1, keepdims=True)
        vt = jnp.concatenate([vp[n * ppb + j][0, 0] for j in range(ppb)], axis=1).astype(BF16)
        acc = acc + lax.dot_general(p_n.astype(BF16), vt, NT_DIMS, preferred_element_type=F32)
    acc = acc / den
    out = jnp.zeros((ts, D_ATTN), F32)
    for h in range(N_HEADS):
        out = out + jnp.where(lane_head == h, acc[h * ts:(h + 1) * ts, :], 0.0)
    o_ref[0] = out


def _attn_sample(q, k_new, v_new, cache_kt, cache_vt, page_table, layer):
    b, ts, _ = q.shape
    n_pages = page_table.shape[1]
    page = cache_kt.shape[3]
    past_len = n_pages * page
    assert ts == SUBLANES and MOBA_BLOCK % page == 0 and past_len % MOBA_BLOCK == 0
    nb = past_len // MOBA_BLOCK
    assert nb <= LANES
    tok = pl.BlockSpec((1, ts, D_ATTN), lambda bi, pt: (bi, 0, 0))

    def page_spec(j):
        return pl.BlockSpec((1, 1, D_ATTN, page), lambda bi, pt: (pt[bi, j], layer, 0, 0))

    pages = [page_spec(j) for j in range(n_pages)]
    return pl.pallas_call(
        functools.partial(_attn_sample_kernel, n_pages=n_pages, k_top=min(MOBA_TOP_K, nb)),
        grid_spec=pltpu.PrefetchScalarGridSpec(
            num_scalar_prefetch=1,
            grid=(b,),
            in_specs=[tok, tok, tok] + pages + pages,
            out_specs=tok,
        ),
        out_shape=jax.ShapeDtypeStruct((b, ts, D_ATTN), F32),
        compiler_params=_params("parallel"),
    )(page_table, q, k_new, v_new, *([cache_kt] * n_pages), *([cache_vt] * n_pages))


def _lane_windows(shape):
    grp = lax.broadcasted_iota(jnp.int32, shape, len(shape) - 1) // (D_POOL // POOL_GROUPS)
    win = jnp.full(shape, POOL_WINDOWS[0], jnp.int32)
    for gi in range(1, POOL_GROUPS):
        win = jnp.where(grp == gi, POOL_WINDOWS[gi], win)
    return win


def _branch_prompt_kernel(ug_ref, vn_ref, xc_ref, prev_ref, ws_ref, bs_ref, wp_ref, ps_ref,
                          b_ref, c_ref):
    ti = pl.program_id(1)
    tm = ug_ref.shape[1]
    dg = D_GMLP // GMLP_GROUPS

    wr = lax.broadcasted_iota(jnp.int32, (GMLP_CHUNK, GMLP_GROUPS * GMLP_CHUNK), 0)
    wc = lax.broadcasted_iota(jnp.int32, (GMLP_CHUNK, GMLP_GROUPS * GMLP_CHUNK), 1) % GMLP_CHUNK
    w_tril = jnp.where(wc <= wr, ws_ref[...], 0.0).astype(BF16)
    lane_grp = lax.broadcasted_iota(jnp.int32, (GMLP_CHUNK, D_GMLP), 1) // dg
    for c in range(tm // GMLP_CHUNK):
        rows = slice(c * GMLP_CHUNK, (c + 1) * GMLP_CHUNK)
        vc = vn_ref[0, rows, :]
        v_stack = jnp.concatenate([jnp.where(lane_grp == gi, vc, 0.0) for gi in range(GMLP_GROUPS)],
                                  axis=0).astype(BF16)
        mixed = jnp.dot(w_tril, v_stack, preferred_element_type=F32) + bs_ref[...]
        b_ref[0, rows, :] = ug_ref[0, rows, :] * mixed

    x = xc_ref[0]
    prev = jnp.where(ti > 0, prev_ref[0], 0.0)
    xe = jnp.concatenate([prev, x], axis=0)
    s2 = xe + pltpu.roll(xe, 1, axis=0)
    s4 = s2 + pltpu.roll(s2, 2, axis=0)
    s8 = s4 + pltpu.roll(s4, 4, axis=0)
    s16 = s8 + pltpu.roll(s8, 8, axis=0)
    win = _lane_windows((tm, D_POOL))
    pre = 2 * SUBLANES
    wsum = jnp.where(win == 2, s2[pre:], jnp.where(win == 4, s4[pre:],
                     jnp.where(win == 8, s8[pre:], s16[pre:])))
    pos = ti * tm + lax.broadcasted_iota(jnp.int32, (tm, D_POOL), 0)
    count = jnp.minimum(win, pos + 1).astype(F32)
    pooled = wsum / count - x
    y = jnp.dot(pooled.astype(BF16), wp_ref[...], preferred_element_type=F32)
    c_ref[0] = y * ps_ref[...]


def _branch_prompt(ug, vn, xc, ws_cat, bs_t, wp_bd, ps, tm):
    b, t, _ = ug.shape
    assert t % tm == 0 and tm % GMLP_CHUNK == 0
    pre = 2 * SUBLANES
    tile = pl.BlockSpec((1, tm, D_GMLP), lambda bi, ti: (bi, ti, 0))
    prev = pl.BlockSpec((1, pre, D_POOL), lambda bi, ti: (bi, jnp.maximum(ti * (tm // pre) - 1, 0), 0))
    return pl.pallas_call(
        _branch_prompt_kernel,
        grid=(b, t // tm),
        in_specs=[tile, tile, tile, prev, _resident(ws_cat.shape), _resident(bs_t.shape),
                  _resident(wp_bd.shape), _resident(ps.shape)],
        out_specs=[tile, tile],
        out_shape=[jax.ShapeDtypeStruct((b, t, D_GMLP), F32), jax.ShapeDtypeStruct((b, t, D_POOL), F32)],
        compiler_params=_params("parallel", "parallel"),
    )(ug, vn, xc, xc, ws_cat, bs_t, wp_bd, ps)


def _branch_sample_kernel(ug_ref, vn_ref, xc_ref, pre_ref, wc_ref, bs_ref, wp_ref, ps_ref,
                          b_ref, c_ref, *, pos0):
    nb, ts, _ = ug_ref.shape
    vn = vn_ref[...]
    t_idx = lax.broadcasted_iota(jnp.int32, (ts, D_GMLP), 0)
    mixed = jnp.zeros((nb, ts, D_GMLP), F32) + bs_ref[...][None]
    for s in range(ts):
        coef = jnp.where(s <= t_idx, wc_ref[s], 0.0)
        mixed = mixed + coef[None] * vn[:, s:s + 1, :]
    b_ref[...] = ug_ref[...] * mixed

    x = xc_ref[...]
    pre = pre_ref[...]
    win = _lane_windows((ts, D_POOL))
    end = POOL_BUF + t_idx
    wsum = jnp.zeros((nb, ts, D_POOL), F32)
    for r in range(POOL_BUF + ts):
        row = pre[:, r:r + 1, :] if r < POOL_BUF else x[:, r - POOL_BUF:r - POOL_BUF + 1, :]
        inside = (r <= end) & (r > end - win)
        wsum = wsum + jnp.where(inside, 1.0, 0.0)[None] * row
    count = jnp.minimum(win, pos0 + t_idx + 1).astype(F32)
    pooled = wsum / count[None] - x
    y = jnp.dot(pooled.reshape(nb * ts, D_POOL).astype(BF16), wp_ref[...], preferred_element_type=F32)
    c_ref[...] = (y * ps_ref[...]).reshape(nb, ts, D_POOL)


def _branch_sample(ug, vn, xc, prefix, wc, bs_t, wp_bd, ps, pos0):
    b, ts, _ = ug.shape
    assert ts == SUBLANES
    full = lambda shape: pl.BlockSpec(shape, lambda i: (0,) * len(shape))
    return pl.pallas_call(
        functools.partial(_branch_sample_kernel, pos0=pos0),
        grid=(1,),
        in_specs=[full(ug.shape), full(vn.shape), full(xc.shape), full(prefix.shape), full(wc.shape),
                  full(bs_t.shape), full(wp_bd.shape), full(ps.shape)],
        out_specs=[full(ug.shape), full(xc.shape)],
        out_shape=[jax.ShapeDtypeStruct(ug.shape, F32), jax.ShapeDtypeStruct(xc.shape, F32)],
        compiler_params=_params("arbitrary"),
    )(ug, vn, xc, prefix, wc, bs_t, wp_bd, ps)


def _merge_kernel(h_ref, a_ref, b_ref, c_ref, gates_ref, wa_ref, wb_ref, wc_ref, wo_ref, o_ref):
    def branch(x_ref, w_ref, j):
        y = jnp.dot(x_ref[...].astype(BF16), w_ref[...], preferred_element_type=F32)
        return gates_ref[:, j * D_MODEL:(j + 1) * D_MODEL].astype(F32) * y

    merged = branch(a_ref, wa_ref, 0) + branch(b_ref, wb_ref, 1) + branch(c_ref, wc_ref, 2)
    o_ref[...] = h_ref[...] + jnp.dot(merged.astype(BF16), wo_ref[...], preferred_element_type=F32)


def _merge(h, a, b, c, gates, wa, wb, wc, wo, layer, tm):
    t = h.shape[0]
    row = lambda w: pl.BlockSpec((tm, w), lambda i: (i, 0))
    return pl.pallas_call(
        _merge_kernel,
        grid=(t // tm,),
        in_specs=[row(D_MODEL), row(D_ATTN), row(D_GMLP), row(D_POOL), row(3 * D_MODEL)]
        + [_layer_resident(w.shape, layer) for w in (wa, wb, wc, wo)],
        out_specs=row(D_MODEL),
        out_shape=jax.ShapeDtypeStruct((t, D_MODEL), F32),
        compiler_params=_params("parallel"),
    )(h, a, b, c, gates, wa, wb, wc, wo)


def _mlp_kernel(h_ref, g_ref, wu_ref, wd_ref, gf_ref, o_ref, *, final, ff_chunk):
    h = h_ref[...]
    xb = _rms(h, g_ref[...]).astype(BF16)
    acc = h
    for c in range(D_FF // ff_chunk):
        cols = slice(c * ff_chunk, (c + 1) * ff_chunk)
        up = jnp.dot(xb, wu_ref[:, cols], preferred_element_type=F32)
        act = jnp.square(jnp.maximum(up, 0.0)).astype(BF16)
        acc = acc + jnp.dot(act, wd_ref[cols, :], preferred_element_type=F32)
    o_ref[...] = _rms(acc, gf_ref[...]) if final else acc


def _mlp(h, g, wu, wd, gf, layer, final, tm):
    t = h.shape[0]
    row = pl.BlockSpec((tm, D_MODEL), lambda i: (i, 0))
    return pl.pallas_call(
        functools.partial(_mlp_kernel, final=final, ff_chunk=1024),
        grid=(t // tm,),
        in_specs=[row, _resident((1, D_MODEL)), _layer_resident(wu.shape, layer),
                  _layer_resident(wd.shape, layer), _resident((1, D_MODEL))],
        out_specs=row,
        out_shape=jax.ShapeDtypeStruct((t, D_MODEL), F32),
        compiler_params=_params("parallel"),
    )(h, g, wu, wd, gf)


def _block_diag(w):
    g, d, e = w.shape
    eye = jnp.eye(g, dtype=w.dtype)
    return (eye[:, None, :, None] * w[:, :, None, :]).reshape(g * d, g * e)


def _pages_channel_major(cache):
    n_phys, depth, page, _, _ = cache.shape
    return jnp.transpose(cache, (0, 1, 3, 4, 2)).reshape(n_phys, depth, D_ATTN, page)


def _heads_last(kt):
    b, depth, _, t = kt.shape
    return jnp.transpose(kt.reshape(b, depth, N_HEADS, HEAD_DIM, t), (0, 1, 4, 2, 3))


def kernel(x_prompt, x_sample, cache_k, cache_v, state_pool, page_table, norm1_g, w_in, gmlp_norm_g,
           w_spatial, b_spatial, w_pool, pool_scale, w_branch_a, w_branch_b, w_branch_c, w_out,
           norm2_g, w_up, w_down, final_norm_g):
    bp, tp, _ = x_prompt.shape
    bs, ts, _ = x_sample.shape
    depth = w_in.shape[0]
    past_len = page_table.shape[1] * cache_k.shape[2]
    dg = D_GMLP // GMLP_GROUPS
    tm = min(512, tp)
    tms = min(512, bs * ts)

    ckt = _pages_channel_major(cache_k)
    cvt = _pages_channel_major(cache_v)
    w_in_b = w_in.astype(BF16)
    wkv_t = jnp.transpose(w_in[:, :, OFF_K:OFF_K + 2 * D_ATTN], (0, 2, 1)).astype(BF16)
    wa_b, wb_b, wc_b = w_branch_a.astype(BF16), w_branch_b.astype(BF16), w_branch_c.astype(BF16)
    wo_b, wu_b, wd_b = w_out.astype(BF16), w_up.astype(BF16), w_down.astype(BF16)
    gf = final_norm_g.reshape(1, D_MODEL)

    hp = x_prompt.reshape(bp * tp, D_MODEL)
    hs = x_sample.reshape(bs * ts, D_MODEL)
    outs = {n: [] for n in ("kp", "vp", "pp", "ks", "vs", "ps", "gv")}
    for l in range(depth):
        g1 = norm1_g[l].reshape(1, D_MODEL)
        g2 = norm2_g[l].reshape(1, D_MODEL)
        gn = gmlp_norm_g[l].reshape(1, D_GMLP)
        ps = pool_scale[l].reshape(1, D_POOL)
        wp_bd = _block_diag(w_pool[l]).astype(BF16)
        ws_cat = jnp.transpose(w_spatial[l], (1, 0, 2)).reshape(GMLP_CHUNK, GMLP_GROUPS * GMLP_CHUNK)
        bs_full = jnp.repeat(b_spatial[l].T, dg, axis=1)
        wc_s = jnp.repeat(jnp.transpose(w_spatial[l][:, :ts, :ts], (2, 1, 0)), dg, axis=2)
        bs_s = bs_full[:ts]

        q, kt, vt, ug, vn, xc, gates = _proj(hp, g1, w_in_b, wkv_t, gn, l, bp, tm, True)
        r3 = lambda a: a.reshape(bp, tp, a.shape[-1])
        a_out = _attn_prompt(r3(q), kt, vt)
        b_out, c_out = _branch_prompt(r3(ug), r3(vn), r3(xc), ws_cat, bs_full, wp_bd, ps, tm)
        h1 = _merge(hp, a_out.reshape(bp * tp, D_ATTN), b_out.reshape(bp * tp, D_GMLP),
                    c_out.reshape(bp * tp, D_POOL), gates, wa_b, wb_b, wc_b, wo_b, l, tm)
        hp = _mlp(h1, g2, wu_b, wd_b, gf, l, l == depth - 1, tm)
        outs["kp"].append(kt)
        outs["vp"].append(vt)
        outs["pp"].append(r3(xc)[:, tp - POOL_BUF:])

        q, k, v, ug, vn, xc, gates = _proj(hs, g1, w_in_b, wkv_t, gn, l, 1, tms, False)
        s3 = lambda a: a.reshape(bs, ts, a.shape[-1])
        a_out = _attn_sample(s3(q), s3(k), s3(v), ckt, cvt, page_table, l)
        prefix = state_pool[:, l]
        b_out, c_out = _branch_sample(s3(ug), s3(vn), s3(xc), prefix, wc_s, bs_s, wp_bd, ps, past_len)
        h1 = _merge(hs, a_out.reshape(bs * ts, D_ATTN), b_out.reshape(bs * ts, D_GMLP),
                    c_out.reshape(bs * ts, D_POOL), gates, wa_b, wb_b, wc_b, wo_b, l, tms)
        hs = _mlp(h1, g2, wu_b, wd_b, gf, l, l == depth - 1, tms)
        outs["ks"].append(k.reshape(bs, ts, N_HEADS, HEAD_DIM))
        outs["vs"].append(v.reshape(bs, ts, N_HEADS, HEAD_DIM))
        outs["ps"].append(jnp.concatenate([prefix, s3(xc)], axis=1)[:, -POOL_BUF:])
        outs["gv"].append(s3(vn))

    stack = lambda n: jnp.stack(outs[n], axis=1)
    return (hp.reshape(bp, tp, D_MODEL), hs.reshape(bs, ts, D_MODEL), _heads_last(stack("kp")),
            _heads_last(stack("vp")), stack("pp"), stack("ks"), stack("vs"), stack("ps"), stack("gv"))
```

```python
import functools

import jax
import jax.numpy as jnp
from jax import lax
from jax.experimental import pallas as pl
from jax.experimental.pallas import tpu as pltpu

F32 = jnp.float32
BF16 = jnp.bfloat16

D_MODEL = 1024
N_HEADS = 8
HEAD_DIM = 64
D_ATTN = N_HEADS * HEAD_DIM
MOBA_BLOCK = 256
MOBA_TOP_K = 3
GMLP_GROUPS = 4
D_GMLP = 256
GMLP_CHUNK = 128
POOL_GROUPS = 4
D_POOL = 256
POOL_WINDOWS = (2, 4, 8, 16)
POOL_BUF = max(POOL_WINDOWS) - 1
D_FF = 4 * D_MODEL
NORM_EPS = 1e-6
N_IN = 3 * D_ATTN + 2 * D_GMLP + D_POOL + 3 * D_MODEL
OFF_Q, OFF_K, OFF_V = 0, D_ATTN, 2 * D_ATTN
OFF_U = 3 * D_ATTN
OFF_VG = OFF_U + D_GMLP
OFF_XC = OFF_VG + D_GMLP
OFF_GATES = OFF_XC + D_POOL

LANES = 128
SUBLANES = 8
HEADS_PER_TILE = LANES // HEAD_DIM
ATTN_SCALE = HEAD_DIM ** -0.5
LOG2E = 1.4426950408889634
MASK_NEG = -(2.0 ** 100)
VMEM_LIMIT = 56 * 1024 * 1024
SAMPLES_PER_STEP = 2

NT_DIMS = (((1,), (1,)), ((), ()))


def _params(*sem):
    return pltpu.CompilerParams(dimension_semantics=sem, vmem_limit_bytes=VMEM_LIMIT)


def _resident(shape):
    nd = len(shape)
    return pl.BlockSpec(shape, lambda *_: (0,) * nd, pipeline_mode=pl.Buffered(1))


def _layer_resident(stacked_shape, layer):
    nd = len(stacked_shape) - 1
    return pl.BlockSpec((None,) + tuple(stacked_shape[1:]), lambda *_: (layer,) + (0,) * nd,
                        pipeline_mode=pl.Buffered(1))


def _rms(x, g):
    return x * lax.rsqrt(jnp.mean(x * x, axis=-1, keepdims=True) + NORM_EPS) * g


def _proj_kernel(h_ref, g_ref, w_ref, wkv_t_ref, gn_ref, *refs, kv_transposed):
    q_ref, k_ref, v_ref, ug_ref, vn_ref, xc_ref, gates_ref = refs[-7:]
    xb = _rms(h_ref[...], g_ref[...]).astype(BF16)

    def seg(lo, width):
        return jnp.dot(xb, w_ref[:, lo:lo + width], preferred_element_type=F32)

    q_ref[...] = seg(OFF_Q, D_ATTN)
    if kv_transposed:
        k_ref[0, 0] = lax.dot_general(wkv_t_ref[0:D_ATTN, :], xb, NT_DIMS,
                                      preferred_element_type=F32)
        v_ref[0, 0] = lax.dot_general(wkv_t_ref[D_ATTN:2 * D_ATTN, :], xb, NT_DIMS,
                                      preferred_element_type=F32)
    else:
        k_ref[...] = seg(OFF_K, D_ATTN)
        v_ref[...] = seg(OFF_V, D_ATTN)
    ug_ref[...] = jax.nn.gelu(seg(OFF_U, D_GMLP))
    gv = jax.nn.gelu(seg(OFF_VG, D_GMLP))
    gc = gv - jnp.mean(gv, axis=-1, keepdims=True)
    vn_ref[...] = gc * lax.rsqrt(jnp.mean(gc * gc, axis=-1, keepdims=True) + NORM_EPS) * gn_ref[...]
    xc_ref[...] = seg(OFF_XC, D_POOL)
    for j in range(3):
        z = seg(OFF_GATES + j * D_MODEL, D_MODEL)
        gates_ref[:, j * D_MODEL:(j + 1) * D_MODEL] = jax.nn.sigmoid(z).astype(BF16)


def _proj(h, g, w_in, wkv_t, gn, layer, nseq, tm, kv_stacks=None):
    n = h.shape[0]
    t = n // nseq
    nt = t // tm
    row = lambda w: pl.BlockSpec((tm, w), lambda bi, ti: (bi * nt + ti, 0))
    flat = lambda w, dt=F32: jax.ShapeDtypeStruct((n, w), dt)
    in_specs = [row(D_MODEL), _resident((1, D_MODEL)), _layer_resident(w_in.shape, layer),
                _layer_resident(wkv_t.shape, layer), _resident((1, D_GMLP))]
    operands = [h, g, w_in, wkv_t, gn]
    if kv_stacks is not None:
        kv_spec = pl.BlockSpec((1, 1, D_ATTN, tm), lambda bi, ti: (bi, layer, 0, ti))
        kv_shape = jax.ShapeDtypeStruct(kv_stacks[0].shape, F32)
        aliases = {len(operands): 1, len(operands) + 1: 2}
        in_specs += [pl.BlockSpec(memory_space=pl.ANY)] * 2
        operands += list(kv_stacks)
    else:
        kv_spec, kv_shape, aliases = row(D_ATTN), flat(D_ATTN), {}
    return pl.pallas_call(
        functools.partial(_proj_kernel, kv_transposed=kv_stacks is not None),
        grid=(nseq, nt),
        in_specs=in_specs,
        out_specs=[row(D_ATTN), kv_spec, kv_spec, row(D_GMLP), row(D_GMLP), row(D_POOL),
                   row(3 * D_MODEL)],
        out_shape=[flat(D_ATTN), kv_shape, kv_shape, flat(D_GMLP), flat(D_GMLP), flat(D_POOL),
                   flat(3 * D_MODEL, BF16)],
        input_output_aliases=aliases,
        compiler_params=_params("parallel", "parallel"),
    )(*operands)


def _attn_prompt_kernel(q_ref, kt_ref, vt_ref, o_ref, kaug_ref, vb_ref, kbd_ref, qaug_ref, s_ref,
                        mx_ref, l_ref, acc_ref, *, k_top):
    i = pl.program_id(1)
    nb, n_pairs = kaug_ref.shape[0], kaug_ref.shape[1]
    nbp = SUBLANES
    fold = lambda x: (x[:, 0:LANES], x[:, LANES:2 * LANES])

    @pl.when(i == 0)
    def _():
        lane = lax.broadcasted_iota(jnp.int32, (LANES, LANES), 1)
        oh_row = lax.broadcasted_iota(jnp.int32, (LANES, MOBA_BLOCK), 0)
        half = lax.broadcasted_iota(jnp.int32, (nbp, LANES), 1) // HEAD_DIM
        kbd_ref[...] = jnp.zeros(kbd_ref.shape, F32)
        for p in range(n_pairs):
            chans = slice(p * LANES, (p + 1) * LANES)
            km_t = jnp.zeros((LANES, LANES), F32)
            for n in range(nb):
                cols = slice(n * MOBA_BLOCK, (n + 1) * MOBA_BLOCK)
                ktn = kt_ref[0, chans, cols]
                kaug_ref[n, p, 0:LANES, :] = ktn.astype(BF16)
                onehot = (oh_row % nbp == n) & (oh_row // (HEADS_PER_TILE * nbp) == p)
                kaug_ref[n, p, LANES:2 * LANES, :] = jnp.where(onehot, 1.0, 0.0).astype(BF16)
                vb_ref[n, p] = vt_ref[0, chans, cols].astype(BF16)
                col = jnp.sum(ktn, axis=1, keepdims=True) * (1.0 / MOBA_BLOCK)
                km_t = jnp.where(lane == n, col, km_t)
            km = km_t.T[0:nbp, :]
            for hh in range(HEADS_PER_TILE):
                h = p * HEADS_PER_TILE + hh
                kbd_ref[h * nbp:(h + 1) * nbp, chans] = jnp.where(half == hh, km, 0.0)

    qf = q_ref[0]
    g = lax.dot_general(kbd_ref[...], qf, NT_DIMS, precision=lax.Precision.HIGHEST,
                        preferred_element_type=F32)
    gm = g.reshape(N_HEADS, nbp, MOBA_BLOCK)
    blk = lax.broadcasted_iota(jnp.int32, gm.shape, 1)
    past = blk < i
    gm = jnp.where(past, gm, -jnp.inf)
    rank = jnp.zeros(gm.shape, jnp.int32)
    for m in range(nb):
        gr = gm[:, m:m + 1, :]
        beats = (gr > gm) | ((gr == gm) & (m < blk))
        rank = rank + beats.astype(jnp.int32)
    sel = past & (rank < k_top) & (gm > -jnp.inf) & (gm < jnp.inf)
    sel_t = jnp.where(sel, 1.0, 0.0).reshape(N_HEADS * nbp, MOBA_BLOCK)
    sel_t = jnp.concatenate([sel_t, jnp.zeros((LANES - N_HEADS * nbp, MOBA_BLOCK), F32)], axis=0)
    rr = lax.broadcasted_iota(jnp.int32, (MOBA_BLOCK, MOBA_BLOCK), 0)
    cc = lax.broadcasted_iota(jnp.int32, (MOBA_BLOCK, MOBA_BLOCK), 1)
    eye = jnp.where(rr == cc, 1.0, 0.0).astype(BF16)
    sel_n = lax.dot_general(eye, sel_t.astype(BF16), NT_DIMS, preferred_element_type=F32)
    bias_all = jnp.where(sel_n > 0.5, 0.0, MASK_NEG)
    lane = lax.broadcasted_iota(jnp.int32, (MOBA_BLOCK, LANES), 1)
    for h in range(N_HEADS):
        p, hh = divmod(h, HEADS_PER_TILE)
        qh = jnp.where(lane // HEAD_DIM == hh, qf[:, p * LANES:(p + 1) * LANES], 0.0)
        qaug_ref[h, :, 0:LANES] = (qh * (ATTN_SCALE * LOG2E)).astype(BF16)
        qaug_ref[h, :, LANES:2 * LANES] = jnp.where(lane // nbp == h, bias_all, 0.0).astype(BF16)

    mx_ref[...] = jnp.full(mx_ref.shape, MASK_NEG, F32)

    def scores(n, carry):
        for h in range(N_HEADS):
            s = jnp.dot(qaug_ref[h], kaug_ref[n, h // HEADS_PER_TILE],
                        preferred_element_type=F32)
            s_ref[h, n] = s
            lo, hi = fold(s)
            mx_ref[h] = jnp.maximum(mx_ref[h], jnp.maximum(lo, hi))
        return carry

    lax.fori_loop(0, i, scores, 0)

    causal = cc <= rr
    for h in range(N_HEADS):
        p = h // HEADS_PER_TILE
        s_own = jnp.dot(qaug_ref[h, :, 0:LANES], kaug_ref[i, p, 0:LANES, :],
                        preferred_element_type=F32)
        s_own = jnp.where(causal, s_own, MASK_NEG)
        lo, hi = fold(s_own)
        mx = jnp.max(jnp.maximum(mx_ref[h], jnp.maximum(lo, hi)), axis=-1, keepdims=True)
        mx_ref[h] = jnp.broadcast_to(mx, (MOBA_BLOCK, LANES))
        p_own = jnp.exp2(s_own - mx)
        lo, hi = fold(p_own)
        l_ref[h] = lo + hi
        acc_ref[h] = lax.dot_general(p_own.astype(BF16), vb_ref[i, p], NT_DIMS,
                                     preferred_element_type=F32)

    def values(n, carry):
        for h in range(N_HEADS):
            mx = mx_ref[h]
            lo, hi = fold(s_ref[h, n])
            p_lo = jnp.exp2(lo - mx)
            p_hi = jnp.exp2(hi - mx)
            l_ref[h] += p_lo + p_hi
            pb = jnp.concatenate([p_lo, p_hi], axis=1).astype(BF16)
            acc_ref[h] += lax.dot_general(pb, vb_ref[n, h // HEADS_PER_TILE], NT_DIMS,
                                          preferred_element_type=F32)
        return carry

    lax.fori_loop(0, i, values, 0)

    for p in range(n_pairs):
        o = [acc_ref[p * HEADS_PER_TILE + hh]
             / jnp.sum(l_ref[p * HEADS_PER_TILE + hh], axis=-1, keepdims=True)
             for hh in range(HEADS_PER_TILE)]
        o_ref[0, :, p * LANES:(p + 1) * LANES] = jnp.where(lane < HEAD_DIM, o[0], o[1])


def _attn_prompt(q, kt, vt, layer):
    b, t, _ = q.shape
    assert t % MOBA_BLOCK == 0
    nb = t // MOBA_BLOCK
    n_pairs = D_ATTN // LANES
    assert nb <= SUBLANES and N_HEADS * SUBLANES <= LANES
    qspec = pl.BlockSpec((1, MOBA_BLOCK, D_ATTN), lambda bi, i: (bi, i, 0))
    kspec = pl.BlockSpec((1, None, D_ATTN, t), lambda bi, i: (bi, layer, 0, 0))
    head_acc = pltpu.VMEM((N_HEADS, MOBA_BLOCK, LANES), F32)
    return pl.pallas_call(
        functools.partial(_attn_prompt_kernel, k_top=min(MOBA_TOP_K, nb)),
        grid=(b, nb),
        in_specs=[qspec, kspec, kspec],
        out_specs=qspec,
        out_shape=jax.ShapeDtypeStruct((b, t, D_ATTN), F32),
        scratch_shapes=[pltpu.VMEM((nb, n_pairs, 2 * LANES, MOBA_BLOCK), BF16),
                        pltpu.VMEM((nb, n_pairs, LANES, MOBA_BLOCK), BF16),
                        pltpu.VMEM((N_HEADS * SUBLANES, D_ATTN), F32),
                        pltpu.VMEM((N_HEADS, MOBA_BLOCK, 2 * LANES), BF16),
                        pltpu.VMEM((N_HEADS, nb, MOBA_BLOCK, MOBA_BLOCK), F32),
                        head_acc, head_acc, head_acc],
        compiler_params=_params("parallel", "arbitrary"),
    )(q, kt, vt)


def _attn_sample_kernel(pt_ref, q_ref, kn_ref, vn_ref, *refs, n_pages, k_top):
    del pt_ref
    o_ref = refs[-1]
    for s in range(q_ref.shape[0]):
        kp = refs[2 * s * n_pages:(2 * s + 1) * n_pages]
        vp = refs[(2 * s + 1) * n_pages:(2 * s + 2) * n_pages]
        o_ref[s] = _attend_sample(q_ref[s], kn_ref[s], vn_ref[s], kp, vp, k_top)


def _attend_sample(q, kn, vn, kp, vp, k_top):
    n_pages = len(kp)
    ts = q.shape[0]
    page = kp[0].shape[3]
    ppb = MOBA_BLOCK // page
    nb = n_pages // ppb
    rows = N_HEADS * ts

    lane_head = lax.broadcasted_iota(jnp.int32, (ts, D_ATTN), 1) // HEAD_DIM
    q_bd = jnp.concatenate([jnp.where(lane_head == h, q, 0.0) for h in range(N_HEADS)], axis=0)
    qs = (q_bd * ATTN_SCALE).astype(BF16)

    lane = lax.broadcasted_iota(jnp.int32, (D_ATTN, LANES), 1)
    km_t = jnp.zeros((D_ATTN, LANES), F32)
    s_list = []
    for n in range(nb):
        kt = jnp.concatenate([kp[n * ppb + j][0, 0] for j in range(ppb)], axis=1)
        col = jnp.sum(kt, axis=1, keepdims=True) * (1.0 / MOBA_BLOCK)
        km_t = jnp.where(lane == n, col, km_t)
        s_list.append(jnp.dot(qs, kt.astype(BF16), preferred_element_type=F32))

    gate = jnp.dot(q_bd, km_t, precision=lax.Precision.HIGHEST, preferred_element_type=F32)
    blk = lax.broadcasted_iota(jnp.int32, (rows, LANES), 1)
    gm = jnp.where(blk < nb, gate, -jnp.inf)
    rank = jnp.zeros((rows, LANES), jnp.int32)
    for m in range(nb):
        gc = gm[:, m:m + 1]
        beats = (gc > gm) | ((gc == gm) & (m < blk))
        rank = rank + beats.astype(jnp.int32)
    sel = (rank < k_top) & (gm > -jnp.inf) & (gm < jnp.inf)
    bias = jnp.where(sel, 0.0, MASK_NEG)

    pad = jnp.zeros((LANES - ts, D_ATTN), F32)
    k_new = jnp.concatenate([kn, pad], axis=0).astype(BF16)
    v_new = jnp.concatenate([vn, pad], axis=0).astype(BF16)
    s_new = lax.dot_general(qs, k_new, NT_DIMS, preferred_element_type=F32)
    qi = lax.broadcasted_iota(jnp.int32, (rows, LANES), 0) % ts
    s_new = jnp.where(blk <= qi, s_new, MASK_NEG)

    mx = jnp.max(s_new, axis=-1, keepdims=True)
    for n in range(nb):
        s_list[n] = s_list[n] + bias[:, n:n + 1]
        mx = jnp.maximum(mx, jnp.max(s_list[n], axis=-1, keepdims=True))
    p_new = jnp.exp(s_new - mx)
    den = jnp.sum(p_new, axis=-1, keepdims=True)
    acc = jnp.dot(p_new.astype(BF16), v_new, preferred_element_type=F32)
    for n in range(nb):
        p_n = jnp.exp(s_list[n] - mx)
        den = den + jnp.sum(p_n, axis=-1, keepdims=True)
        vt = jnp.concatenate([vp[n * ppb + j][0, 0] for j in range(ppb)], axis=1).astype(BF16)
        acc = acc + lax.dot_general(p_n.astype(BF16), vt, NT_DIMS, preferred_element_type=F32)
    acc = acc / den
    out = jnp.zeros((ts, D_ATTN), F32)
    for h in range(N_HEADS):
        out = out + jnp.where(lane_head == h, acc[h * ts:(h + 1) * ts, :], 0.0)
    return out


def _attn_sample(q, k_new, v_new, cache_kt, cache_vt, page_table, layer):
    b, ts, _ = q.shape
    n_pages = page_table.shape[1]
    page = cache_kt.shape[3]
    past_len = n_pages * page
    assert ts == SUBLANES and MOBA_BLOCK % page == 0 and past_len % MOBA_BLOCK == 0
    nb = past_len // MOBA_BLOCK
    assert nb <= LANES
    spb = SAMPLES_PER_STEP if b % SAMPLES_PER_STEP == 0 else 1
    tok = pl.BlockSpec((spb, ts, D_ATTN), lambda bi, pt: (bi, 0, 0))

    def page_spec(s, j):
        return pl.BlockSpec((1, 1, D_ATTN, page), lambda bi, pt: (pt[bi * spb + s, j], layer, 0, 0))

    pages, caches = [], []
    for s in range(spb):
        pages += [page_spec(s, j) for j in range(n_pages)] * 2
        caches += [cache_kt] * n_pages + [cache_vt] * n_pages
    return pl.pallas_call(
        functools.partial(_attn_sample_kernel, n_pages=n_pages, k_top=min(MOBA_TOP_K, nb)),
        grid_spec=pltpu.PrefetchScalarGridSpec(
            num_scalar_prefetch=1,
            grid=(b // spb,),
            in_specs=[tok, tok, tok] + pages,
            out_specs=tok,
        ),
        out_shape=jax.ShapeDtypeStruct((b, ts, D_ATTN), F32),
        compiler_params=_params("parallel"),
    )(page_table, q, k_new, v_new, *caches)


def _lane_windows(shape):
    grp = lax.broadcasted_iota(jnp.int32, shape, len(shape) - 1) // (D_POOL // POOL_GROUPS)
    win = jnp.full(shape, POOL_WINDOWS[0], jnp.int32)
    for gi in range(1, POOL_GROUPS):
        win = jnp.where(grp == gi, POOL_WINDOWS[gi], win)
    return win


def _branch_prompt_kernel(ug_ref, vn_ref, xc_ref, prev_ref, ws_ref, bs_ref, wp_ref, ps_ref,
                          b_ref, c_ref):
    ti = pl.program_id(1)
    tm = ug_ref.shape[1]
    dg = D_GMLP // GMLP_GROUPS

    wr = lax.broadcasted_iota(jnp.int32, (GMLP_CHUNK, GMLP_GROUPS * GMLP_CHUNK), 0)
    wc = lax.broadcasted_iota(jnp.int32, (GMLP_CHUNK, GMLP_GROUPS * GMLP_CHUNK), 1) % GMLP_CHUNK
    w_tril = jnp.where(wc <= wr, ws_ref[...], 0.0).astype(BF16)
    lane_grp = lax.broadcasted_iota(jnp.int32, (GMLP_CHUNK, D_GMLP), 1) // dg
    for c in range(tm // GMLP_CHUNK):
        rows = slice(c * GMLP_CHUNK, (c + 1) * GMLP_CHUNK)
        vc = vn_ref[0, rows, :]
        v_stack = jnp.concatenate([jnp.where(lane_grp == gi, vc, 0.0) for gi in range(GMLP_GROUPS)],
                                  axis=0).astype(BF16)
        mixed = jnp.dot(w_tril, v_stack, preferred_element_type=F32) + bs_ref[...]
        b_ref[0, rows, :] = ug_ref[0, rows, :] * mixed

    x = xc_ref[0]
    prev = jnp.where(ti > 0, prev_ref[0], 0.0)
    xe = jnp.concatenate([prev, x], axis=0)
    s2 = xe + pltpu.roll(xe, 1, axis=0)
    s4 = s2 + pltpu.roll(s2, 2, axis=0)
    s8 = s4 + pltpu.roll(s4, 4, axis=0)
    s16 = s8 + pltpu.roll(s8, 8, axis=0)
    win = _lane_windows((tm, D_POOL))
    pre = 2 * SUBLANES
    wsum = jnp.where(win == 2, s2[pre:], jnp.where(win == 4, s4[pre:],
                     jnp.where(win == 8, s8[pre:], s16[pre:])))
    pos = ti * tm + lax.broadcasted_iota(jnp.int32, (tm, D_POOL), 0)
    count = jnp.minimum(win, pos + 1).astype(F32)
    pooled = wsum / count - x
    y = jnp.dot(pooled.astype(BF16), wp_ref[...], preferred_element_type=F32)
    c_ref[0] = y * ps_ref[...]


def _branch_prompt(ug, vn, xc, ws_cat, bs_t, wp_bd, ps, tm):
    b, t, _ = ug.shape
    assert t % tm == 0 and tm % GMLP_CHUNK == 0
    pre = 2 * SUBLANES
    tile = pl.BlockSpec((1, tm, D_GMLP), lambda bi, ti: (bi, ti, 0))
    prev = pl.BlockSpec((1, pre, D_POOL), lambda bi, ti: (bi, jnp.maximum(ti * (tm // pre) - 1, 0), 0))
    return pl.pallas_call(
        _branch_prompt_kernel,
        grid=(b, t // tm),
        in_specs=[tile, tile, tile, prev, _resident(ws_cat.shape), _resident(bs_t.shape),
                  _resident(wp_bd.shape), _resident(ps.shape)],
        out_specs=[tile, tile],
        out_shape=[jax.ShapeDtypeStruct((b, t, D_GMLP), F32), jax.ShapeDtypeStruct((b, t, D_POOL), F32)],
        compiler_params=_params("parallel", "parallel"),
    )(ug, vn, xc, xc, ws_cat, bs_t, wp_bd, ps)


def _branch_sample_kernel(ug_ref, vn_ref, xc_ref, pre_ref, wc_ref, bs_ref, wp_ref, ps_ref,
                          b_ref, c_ref, *, pos0):
    nb, ts, _ = ug_ref.shape
    vn = vn_ref[...]
    t_idx = lax.broadcasted_iota(jnp.int32, (ts, D_GMLP), 0)
    mixed = jnp.zeros((nb, ts, D_GMLP), F32) + bs_ref[...][None]
    for s in range(ts):
        coef = jnp.where(s <= t_idx, wc_ref[s], 0.0)
        mixed = mixed + coef[None] * vn[:, s:s + 1, :]
    b_ref[...] = ug_ref[...] * mixed

    x = xc_ref[...]
    pre = pre_ref[...]
    win = _lane_windows((ts, D_POOL))
    end = POOL_BUF + t_idx
    wsum = jnp.zeros((nb, ts, D_POOL), F32)
    for r in range(POOL_BUF + ts):
        row = pre[:, r:r + 1, :] if r < POOL_BUF else x[:, r - POOL_BUF:r - POOL_BUF + 1, :]
        inside = (r <= end) & (r > end - win)
        wsum = wsum + jnp.where(inside, 1.0, 0.0)[None] * row
    count = jnp.minimum(win, pos0 + t_idx + 1).astype(F32)
    pooled = wsum / count[None] - x
    y = jnp.dot(pooled.reshape(nb * ts, D_POOL).astype(BF16), wp_ref[...], preferred_element_type=F32)
    c_ref[...] = (y * ps_ref[...]).reshape(nb, ts, D_POOL)


def _branch_sample(ug, vn, xc, prefix, wc, bs_t, wp_bd, ps, pos0):
    b, ts, _ = ug.shape
    assert ts == SUBLANES
    full = lambda shape: pl.BlockSpec(shape, lambda i: (0,) * len(shape))
    return pl.pallas_call(
        functools.partial(_branch_sample_kernel, pos0=pos0),
        grid=(1,),
        in_specs=[full(ug.shape), full(vn.shape), full(xc.shape), full(prefix.shape), full(wc.shape),
                  full(bs_t.shape), full(wp_bd.shape), full(ps.shape)],
        out_specs=[full(ug.shape), full(xc.shape)],
        out_shape=[jax.ShapeDtypeStruct(ug.shape, F32), jax.ShapeDtypeStruct(xc.shape, F32)],
        compiler_params=_params("arbitrary"),
    )(ug, vn, xc, prefix, wc, bs_t, wp_bd, ps)


def _merge_mlp_kernel(h_ref, a_ref, b_ref, c_ref, gates_ref, wa_ref, wb_ref, wc_ref, wo_ref, g_ref,
                      wu_ref, wd_ref, gf_ref, o_ref, *, final, ff_chunk):
    def branch(x_ref, w_ref, j):
        y = jnp.dot(x_ref[...].astype(BF16), w_ref[...], preferred_element_type=F32)
        return gates_ref[:, j * D_MODEL:(j + 1) * D_MODEL].astype(F32) * y

    merged = branch(a_ref, wa_ref, 0) + branch(b_ref, wb_ref, 1) + branch(c_ref, wc_ref, 2)
    h1 = h_ref[...] + jnp.dot(merged.astype(BF16), wo_ref[...], preferred_element_type=F32)
    xb = _rms(h1, g_ref[...]).astype(BF16)
    acc = h1
    for c in range(D_FF // ff_chunk):
        cols = slice(c * ff_chunk, (c + 1) * ff_chunk)
        up = jnp.dot(xb, wu_ref[:, cols], preferred_element_type=F32)
        act = jnp.square(jnp.maximum(up, 0.0)).astype(BF16)
        acc = acc + jnp.dot(act, wd_ref[cols, :], preferred_element_type=F32)
    o_ref[...] = _rms(acc, gf_ref[...]) if final else acc


def _merge_mlp(h, a, b, c, gates, wa, wb, wc, wo, g, wu, wd, gf, layer, final, tm):
    t = h.shape[0]
    row = lambda w: pl.BlockSpec((tm, w), lambda i: (i, 0))
    stack = lambda w: _layer_resident(w.shape, layer)
    return pl.pallas_call(
        functools.partial(_merge_mlp_kernel, final=final, ff_chunk=1024),
        grid=(t // tm,),
        in_specs=[row(D_MODEL), row(D_ATTN), row(D_GMLP), row(D_POOL), row(3 * D_MODEL),
                  stack(wa), stack(wb), stack(wc), stack(wo), _resident((1, D_MODEL)),
                  stack(wu), stack(wd), _resident((1, D_MODEL))],
        out_specs=row(D_MODEL),
        out_shape=jax.ShapeDtypeStruct((t, D_MODEL), F32),
        compiler_params=_params("parallel"),
    )(h, a, b, c, gates, wa, wb, wc, wo, g, wu, wd, gf)


def _block_diag(w):
    g, d, e = w.shape
    eye = jnp.eye(g, dtype=w.dtype)
    return (eye[:, None, :, None] * w[:, :, None, :]).reshape(g * d, g * e)


def _pages_channel_major(cache):
    n_phys, depth, page, _, _ = cache.shape
    return jnp.transpose(cache, (0, 1, 3, 4, 2)).reshape(n_phys, depth, D_ATTN, page)


def _heads_last(kt):
    b, depth, _, t = kt.shape
    return jnp.transpose(kt.reshape(b, depth, N_HEADS, HEAD_DIM, t), (0, 1, 4, 2, 3))


def kernel(x_prompt, x_sample, cache_k, cache_v, state_pool, page_table, norm1_g, w_in, gmlp_norm_g,
           w_spatial, b_spatial, w_pool, pool_scale, w_branch_a, w_branch_b, w_branch_c, w_out,
           norm2_g, w_up, w_down, final_norm_g):
    bp, tp, _ = x_prompt.shape
    bs, ts, _ = x_sample.shape
    depth = w_in.shape[0]
    past_len = page_table.shape[1] * cache_k.shape[2]
    dg = D_GMLP // GMLP_GROUPS
    tm = min(512, tp)
    tms = min(512, bs * ts)

    ckt = _pages_channel_major(cache_k)
    cvt = _pages_channel_major(cache_v)
    w_in_b = lax.optimization_barrier(w_in.astype(BF16))
    wkv_t = jnp.transpose(w_in_b[:, :, OFF_K:OFF_K + 2 * D_ATTN], (0, 2, 1))
    wa_b, wb_b, wc_b = w_branch_a.astype(BF16), w_branch_b.astype(BF16), w_branch_c.astype(BF16)
    wo_b, wu_b, wd_b = w_out.astype(BF16), w_up.astype(BF16), w_down.astype(BF16)
    gf = final_norm_g.reshape(1, D_MODEL)

    hp = x_prompt.reshape(bp * tp, D_MODEL)
    hs = x_sample.reshape(bs * ts, D_MODEL)
    outs = {n: [] for n in ("pp", "ks", "vs", "ps", "gv")}
    kv_stacks = (jnp.zeros((bp, depth, D_ATTN, tp), F32), jnp.zeros((bp, depth, D_ATTN, tp), F32))
    for l in range(depth):
        g1 = norm1_g[l].reshape(1, D_MODEL)
        g2 = norm2_g[l].reshape(1, D_MODEL)
        gn = gmlp_norm_g[l].reshape(1, D_GMLP)
        ps = pool_scale[l].reshape(1, D_POOL)
        wp_bd = _block_diag(w_pool[l]).astype(BF16)
        ws_cat = jnp.transpose(w_spatial[l], (1, 0, 2)).reshape(GMLP_CHUNK, GMLP_GROUPS * GMLP_CHUNK)
        bs_full = jnp.repeat(b_spatial[l].T, dg, axis=1)
        wc_s = jnp.repeat(jnp.transpose(w_spatial[l][:, :ts, :ts], (2, 1, 0)), dg, axis=2)
        bs_s = bs_full[:ts]

        q, kst, vst, ug, vn, xc, gates = _proj(hp, g1, w_in_b, wkv_t, gn, l, bp, tm, kv_stacks)
        kv_stacks = (kst, vst)
        r3 = lambda a: a.reshape(bp, tp, a.shape[-1])
        a_out = _attn_prompt(r3(q), kst, vst, l)
        b_out, c_out = _branch_prompt(r3(ug), r3(vn), r3(xc), ws_cat, bs_full, wp_bd, ps, tm)
        hp = _merge_mlp(hp, a_out.reshape(bp * tp, D_ATTN), b_out.reshape(bp * tp, D_GMLP),
                        c_out.reshape(bp * tp, D_POOL), gates, wa_b, wb_b, wc_b, wo_b, g2, wu_b, wd_b,
                        gf, l, l == depth - 1, tm)
        outs["pp"].append(r3(xc)[:, tp - POOL_BUF:])

        q, k, v, ug, vn, xc, gates = _proj(hs, g1, w_in_b, wkv_t, gn, l, 1, tms)
        s3 = lambda a: a.reshape(bs, ts, a.shape[-1])
        a_out = _attn_sample(s3(q), s3(k), s3(v), ckt, cvt, page_table, l)
        prefix = state_pool[:, l]
        b_out, c_out = _branch_sample(s3(ug), s3(vn), s3(xc), prefix, wc_s, bs_s, wp_bd, ps, past_len)
        hs = _merge_mlp(hs, a_out.reshape(bs * ts, D_ATTN), b_out.reshape(bs * ts, D_GMLP),
                        c_out.reshape(bs * ts, D_POOL), gates, wa_b, wb_b, wc_b, wo_b, g2, wu_b, wd_b,
                        gf, l, l == depth - 1, tms)
        outs["ks"].append(k.reshape(bs, ts, N_HEADS, HEAD_DIM))
        outs["vs"].append(v.reshape(bs, ts, N_HEADS, HEAD_DIM))
        outs["ps"].append(jnp.concatenate([prefix, s3(xc)], axis=1)[:, -POOL_BUF:])
        outs["gv"].append(s3(vn))

    stack = lambda n: jnp.stack(outs[n], axis=1)
    return (hp.reshape(bp, tp, D_MODEL), hs.reshape(bs, ts, D_MODEL), _heads_last(kv_stacks[0]),
            _heads_last(kv_stacks[1]), stack("pp"), stack("ks"), stack("vs"), stack("ps"), stack("gv"))
```

```python
import functools

import jax
import jax.numpy as jnp
from jax import lax
from jax.experimental import pallas as pl
from jax.experimental.pallas import tpu as pltpu

F32 = jnp.float32
BF16 = jnp.bfloat16

D_MODEL = 1024
N_HEADS = 8
HEAD_DIM = 64
D_ATTN = N_HEADS * HEAD_DIM
MOBA_BLOCK = 256
MOBA_TOP_K = 3
GMLP_GROUPS = 4
D_GMLP = 256
GMLP_CHUNK = 128
POOL_GROUPS = 4
D_POOL = 256
POOL_WINDOWS = (2, 4, 8, 16)
POOL_BUF = max(POOL_WINDOWS) - 1
D_FF = 4 * D_MODEL
NORM_EPS = 1e-6
N_IN = 3 * D_ATTN + 2 * D_GMLP + D_POOL + 3 * D_MODEL
OFF_Q, OFF_K, OFF_V = 0, D_ATTN, 2 * D_ATTN
OFF_U = 3 * D_ATTN
OFF_VG = OFF_U + D_GMLP
OFF_XC = OFF_VG + D_GMLP
OFF_GATES = OFF_XC + D_POOL

LANES = 128
SUBLANES = 8
HEADS_PER_TILE = LANES // HEAD_DIM
ATTN_SCALE = HEAD_DIM ** -0.5
LOG2E = 1.4426950408889634
MASK_NEG = -(2.0 ** 100)
VMEM_LIMIT = 56 * 1024 * 1024

NT_DIMS = (((1,), (1,)), ((), ()))


def _params(*sem):
    return pltpu.CompilerParams(dimension_semantics=sem, vmem_limit_bytes=VMEM_LIMIT)


def _resident(shape):
    nd = len(shape)
    return pl.BlockSpec(shape, lambda *_: (0,) * nd, pipeline_mode=pl.Buffered(1))


def _layer_resident(stacked_shape, layer):
    nd = len(stacked_shape) - 1
    return pl.BlockSpec((None,) + tuple(stacked_shape[1:]), lambda *_: (layer,) + (0,) * nd,
                        pipeline_mode=pl.Buffered(1))


def _split_bf16(x):
    hi = x.astype(BF16)
    return hi, (x - hi.astype(F32)).astype(BF16)


def _rms(x, g):
    return x * lax.rsqrt(jnp.mean(x * x, axis=-1, keepdims=True) + NORM_EPS) * g


def _proj_kernel(h_ref, g_ref, w_ref, wkv_t_ref, gn_ref, *refs, kv_transposed):
    q_ref, k_ref, v_ref, ug_ref, vn_ref, xc_ref, gates_ref = refs[-7:]
    xb = _rms(h_ref[...], g_ref[...]).astype(BF16)

    def seg(lo, width):
        return jnp.dot(xb, w_ref[:, lo:lo + width], preferred_element_type=F32)

    q_ref[...] = seg(OFF_Q, D_ATTN)
    if kv_transposed:
        k_ref[0, 0] = lax.dot_general(wkv_t_ref[0:D_ATTN, :], xb, NT_DIMS,
                                      preferred_element_type=F32)
        v_ref[0, 0] = lax.dot_general(wkv_t_ref[D_ATTN:2 * D_ATTN, :], xb, NT_DIMS,
                                      preferred_element_type=F32)
    else:
        k_ref[...] = seg(OFF_K, D_ATTN)
        v_ref[...] = seg(OFF_V, D_ATTN)
    ug_ref[...] = jax.nn.gelu(seg(OFF_U, D_GMLP))
    gv = jax.nn.gelu(seg(OFF_VG, D_GMLP))
    gc = gv - jnp.mean(gv, axis=-1, keepdims=True)
    vn_ref[...] = gc * lax.rsqrt(jnp.mean(gc * gc, axis=-1, keepdims=True) + NORM_EPS) * gn_ref[...]
    xc_ref[...] = seg(OFF_XC, D_POOL)
    for j in range(3):
        z = seg(OFF_GATES + j * D_MODEL, D_MODEL)
        gates_ref[:, j * D_MODEL:(j + 1) * D_MODEL] = jax.nn.sigmoid(z).astype(BF16)


def _proj(h, g, w_in, wkv_t, gn, layer, nseq, tm, kv_stacks=None):
    n = h.shape[0]
    t = n // nseq
    nt = t // tm
    row = lambda w: pl.BlockSpec((tm, w), lambda bi, ti: (bi * nt + ti, 0))
    flat = lambda w, dt=F32: jax.ShapeDtypeStruct((n, w), dt)
    in_specs = [row(D_MODEL), _resident((1, D_MODEL)), _layer_resident(w_in.shape, layer),
                _layer_resident(wkv_t.shape, layer), _resident((1, D_GMLP))]
    operands = [h, g, w_in, wkv_t, gn]
    if kv_stacks is not None:
        kv_spec = pl.BlockSpec((1, 1, D_ATTN, tm), lambda bi, ti: (bi, layer, 0, ti))
        kv_shape = jax.ShapeDtypeStruct(kv_stacks[0].shape, F32)
        aliases = {len(operands): 1, len(operands) + 1: 2}
        in_specs += [pl.BlockSpec(memory_space=pl.ANY)] * 2
        operands += list(kv_stacks)
    else:
        kv_spec, kv_shape, aliases = row(D_ATTN), flat(D_ATTN), {}
    return pl.pallas_call(
        functools.partial(_proj_kernel, kv_transposed=kv_stacks is not None),
        grid=(nseq, nt),
        in_specs=in_specs,
        out_specs=[row(D_ATTN), kv_spec, kv_spec, row(D_GMLP), row(D_GMLP), row(D_POOL),
                   row(3 * D_MODEL)],
        out_shape=[flat(D_ATTN), kv_shape, kv_shape, flat(D_GMLP), flat(D_GMLP), flat(D_POOL),
                   flat(3 * D_MODEL, BF16)],
        input_output_aliases=aliases,
        compiler_params=_params("parallel", "parallel"),
    )(*operands)


def _transpose_cast_kernel(w_ref, o_ref):
    o_ref[0] = w_ref[0].T.astype(BF16)


def _kv_weights_transposed(w_in):
    depth = w_in.shape[0]
    assert OFF_K % D_ATTN == 0 and OFF_V == OFF_K + D_ATTN
    return pl.pallas_call(
        _transpose_cast_kernel,
        grid=(depth, 2),
        in_specs=[pl.BlockSpec((1, D_MODEL, D_ATTN), lambda l, j: (l, 0, OFF_K // D_ATTN + j))],
        out_specs=pl.BlockSpec((1, D_ATTN, D_MODEL), lambda l, j: (l, j, 0)),
        out_shape=jax.ShapeDtypeStruct((depth, 2 * D_ATTN, D_MODEL), BF16),
        compiler_params=_params("parallel", "parallel"),
    )(w_in)


def _attn_prompt_kernel(q_ref, kt_ref, vt_ref, o_ref, kaug_ref, vb_ref, kbd_ref, qaug_ref, s_ref,
                        mx_ref, l_ref, acc_ref, *, k_top):
    i = pl.program_id(1)
    nb, n_pairs = kaug_ref.shape[0], kaug_ref.shape[1]
    nbp = SUBLANES
    fold = lambda x: (x[:, 0:LANES], x[:, LANES:2 * LANES])

    @pl.when(i == 0)
    def _():
        lane = lax.broadcasted_iota(jnp.int32, (LANES, LANES), 1)
        oh_row = lax.broadcasted_iota(jnp.int32, (LANES, MOBA_BLOCK), 0)
        half = lax.broadcasted_iota(jnp.int32, (nbp, LANES), 1) // HEAD_DIM
        kbd_ref[...] = jnp.zeros(kbd_ref.shape, BF16)
        for p in range(n_pairs):
            chans = slice(p * LANES, (p + 1) * LANES)
            km_t = jnp.zeros((LANES, LANES), F32)
            for n in range(nb):
                cols = slice(n * MOBA_BLOCK, (n + 1) * MOBA_BLOCK)
                ktn = kt_ref[0, chans, cols]
                kaug_ref[n, p, 0:LANES, :] = ktn.astype(BF16)
                onehot = (oh_row % nbp == n) & (oh_row // (HEADS_PER_TILE * nbp) == p)
                kaug_ref[n, p, LANES:2 * LANES, :] = jnp.where(onehot, 1.0, 0.0).astype(BF16)
                vb_ref[n, p] = vt_ref[0, chans, cols].astype(BF16)
                col = jnp.sum(ktn, axis=1, keepdims=True) * (1.0 / MOBA_BLOCK)
                km_t = jnp.where(lane == n, col, km_t)
            km = km_t.T
            km = jnp.concatenate([jnp.where(half == hh, km[0:nbp], 0.0)
                                  for hh in range(HEADS_PER_TILE)], axis=0)
            rows = slice(p * HEADS_PER_TILE * nbp, (p + 1) * HEADS_PER_TILE * nbp)
            km_hi, km_lo = _split_bf16(km)
            kbd_ref[0, rows, chans] = km_hi
            kbd_ref[1, rows, chans] = km_lo

    qf = q_ref[0]
    q_hi, q_lo = _split_bf16(qf)
    nt_dot = lambda a, b: lax.dot_general(a, b, NT_DIMS, preferred_element_type=F32)
    g = nt_dot(kbd_ref[0], q_hi) + (nt_dot(kbd_ref[0], q_lo) + nt_dot(kbd_ref[1], q_hi))
    gm = g.reshape(N_HEADS, nbp, MOBA_BLOCK)
    blk = lax.broadcasted_iota(jnp.int32, gm.shape, 1)
    past = blk < i
    gm = jnp.where(past, gm, -jnp.inf)
    rank = jnp.zeros(gm.shape, jnp.int32)
    for m in range(nb):
        gr = gm[:, m:m + 1, :]
        beats = (gr > gm) | ((gr == gm) & (m < blk))
        rank = rank + beats.astype(jnp.int32)
    sel = past & (rank < k_top) & (gm > -jnp.inf) & (gm < jnp.inf)
    sel_t = jnp.where(sel, 1.0, 0.0).reshape(N_HEADS * nbp, MOBA_BLOCK)
    sel_t = jnp.concatenate([sel_t, jnp.zeros((LANES - N_HEADS * nbp, MOBA_BLOCK), F32)], axis=0)
    rr = lax.broadcasted_iota(jnp.int32, (MOBA_BLOCK, MOBA_BLOCK), 0)
    cc = lax.broadcasted_iota(jnp.int32, (MOBA_BLOCK, MOBA_BLOCK), 1)
    eye = jnp.where(rr == cc, 1.0, 0.0).astype(BF16)
    sel_n = lax.dot_general(eye, sel_t.astype(BF16), NT_DIMS, preferred_element_type=F32)
    bias_all = jnp.where(sel_n > 0.5, 0.0, MASK_NEG)
    lane = lax.broadcasted_iota(jnp.int32, (MOBA_BLOCK, LANES), 1)
    for h in range(N_HEADS):
        p, hh = divmod(h, HEADS_PER_TILE)
        qh = jnp.where(lane // HEAD_DIM == hh, qf[:, p * LANES:(p + 1) * LANES], 0.0)
        qaug_ref[h, :, 0:LANES] = (qh * (ATTN_SCALE * LOG2E)).astype(BF16)
        qaug_ref[h, :, LANES:2 * LANES] = jnp.where(lane // nbp == h, bias_all, 0.0).astype(BF16)

    mx_ref[...] = jnp.full(mx_ref.shape, MASK_NEG, F32)

    def scores(n, carry):
        for h in range(N_HEADS):
            s = jnp.dot(qaug_ref[h], kaug_ref[n, h // HEADS_PER_TILE],
                        preferred_element_type=F32)
            s_ref[h, n] = s
            lo, hi = fold(s)
            mx_ref[h] = jnp.maximum(mx_ref[h], jnp.maximum(lo, hi))
        return carry

    lax.fori_loop(0, i, scores, 0)

    causal = cc <= rr
    for h in range(N_HEADS):
        p = h // HEADS_PER_TILE
        s_own = jnp.dot(qaug_ref[h, :, 0:LANES], kaug_ref[i, p, 0:LANES, :],
                        preferred_element_type=F32)
        s_own = jnp.where(causal, s_own, MASK_NEG)
        lo, hi = fold(s_own)
        mx = jnp.max(jnp.maximum(mx_ref[h], jnp.maximum(lo, hi)), axis=-1, keepdims=True)
        mx_ref[h] = jnp.broadcast_to(mx, (MOBA_BLOCK, LANES))
        p_own = jnp.exp2(s_own - mx)
        lo, hi = fold(p_own)
        l_ref[h] = lo + hi
        acc_ref[h] = lax.dot_general(p_own.astype(BF16), vb_ref[i, p], NT_DIMS,
                                     preferred_element_type=F32)

    def values(n, carry):
        for h in range(N_HEADS):
            mx = mx_ref[h]
            lo, hi = fold(s_ref[h, n])
            p_lo = jnp.exp2(lo - mx)
            p_hi = jnp.exp2(hi - mx)
            l_ref[h] += p_lo + p_hi
            pb = jnp.concatenate([p_lo, p_hi], axis=1).astype(BF16)
            acc_ref[h] += lax.dot_general(pb, vb_ref[n, h // HEADS_PER_TILE], NT_DIMS,
                                          preferred_element_type=F32)
        return carry

    lax.fori_loop(0, i, values, 0)

    for p in range(n_pairs):
        o = [acc_ref[p * HEADS_PER_TILE + hh]
             / jnp.sum(l_ref[p * HEADS_PER_TILE + hh], axis=-1, keepdims=True)
             for hh in range(HEADS_PER_TILE)]
        o_ref[0, :, p * LANES:(p + 1) * LANES] = jnp.where(lane < HEAD_DIM, o[0], o[1])


def _attn_prompt(q, kt, vt, layer):
    b, t, _ = q.shape
    assert t % MOBA_BLOCK == 0
    nb = t // MOBA_BLOCK
    n_pairs = D_ATTN // LANES
    assert nb <= SUBLANES and N_HEADS * SUBLANES <= LANES
    qspec = pl.BlockSpec((1, MOBA_BLOCK, D_ATTN), lambda bi, i: (bi, i, 0))
    kspec = pl.BlockSpec((1, None, D_ATTN, t), lambda bi, i: (bi, layer, 0, 0))
    head_acc = pltpu.VMEM((N_HEADS, MOBA_BLOCK, LANES), F32)
    return pl.pallas_call(
        functools.partial(_attn_prompt_kernel, k_top=min(MOBA_TOP_K, nb)),
        grid=(b, nb),
        in_specs=[qspec, kspec, kspec],
        out_specs=qspec,
        out_shape=jax.ShapeDtypeStruct((b, t, D_ATTN), F32),
        scratch_shapes=[pltpu.VMEM((nb, n_pairs, 2 * LANES, MOBA_BLOCK), BF16),
                        pltpu.VMEM((nb, n_pairs, LANES, MOBA_BLOCK), BF16),
                        pltpu.VMEM((2, N_HEADS * SUBLANES, D_ATTN), BF16),
                        pltpu.VMEM((N_HEADS, MOBA_BLOCK, 2 * LANES), BF16),
                        pltpu.VMEM((N_HEADS, nb, MOBA_BLOCK, MOBA_BLOCK), F32),
                        head_acc, head_acc, head_acc],
        compiler_params=_params("parallel", "arbitrary"),
    )(q, kt, vt)


def _attn_sample_kernel(pt_ref, q_ref, kn_ref, vn_ref, ck_hbm, cv_hbm, o_ref, kbuf, vbuf, sem, *,
                        layer, k_top):
    b = pl.program_id(0)
    n_pages = pt_ref.shape[1]
    page = ck_hbm.shape[3]

    def page_copies(sample, slot):
        copies = []
        for j in range(n_pages):
            pid = pt_ref[sample, j]
            cols = pl.ds(j * page, page)
            copies.append(pltpu.make_async_copy(ck_hbm.at[pid, layer], kbuf.at[slot, :, cols],
                                                sem.at[0, slot]))
            copies.append(pltpu.make_async_copy(cv_hbm.at[pid, layer], vbuf.at[slot, :, cols],
                                                sem.at[1, slot]))
        return copies

    slot = b % 2

    @pl.when(b == 0)
    def _():
        for c in page_copies(0, 0):
            c.start()

    @pl.when(b + 1 < pl.num_programs(0))
    def _():
        for c in page_copies(b + 1, 1 - slot):
            c.start()

    for c in page_copies(b, slot):
        c.wait()
    o_ref[0] = _attend_sample(q_ref[0], kn_ref[0], vn_ref[0], kbuf.at[slot], vbuf.at[slot], k_top)


def _attend_sample(q, kn, vn, kt_ref, vt_ref, k_top):
    ts = q.shape[0]
    nb = kt_ref.shape[1] // MOBA_BLOCK
    rows = N_HEADS * ts

    lane_head = lax.broadcasted_iota(jnp.int32, (ts, D_ATTN), 1) // HEAD_DIM
    q_bd = jnp.concatenate([jnp.where(lane_head == h, q, 0.0) for h in range(N_HEADS)], axis=0)
    qs = (q_bd * ATTN_SCALE).astype(BF16)

    lane = lax.broadcasted_iota(jnp.int32, (D_ATTN, LANES), 1)
    km_t = jnp.zeros((D_ATTN, LANES), F32)
    s_list = []
    for n in range(nb):
        kt = kt_ref[:, n * MOBA_BLOCK:(n + 1) * MOBA_BLOCK]
        col = jnp.sum(kt, axis=1, keepdims=True) * (1.0 / MOBA_BLOCK)
        km_t = jnp.where(lane == n, col, km_t)
        s_list.append(jnp.dot(qs, kt.astype(BF16), preferred_element_type=F32))

    gate = jnp.dot(q_bd, km_t, precision=lax.Precision.HIGHEST, preferred_element_type=F32)
    blk = lax.broadcasted_iota(jnp.int32, (rows, LANES), 1)
    gm = jnp.where(blk < nb, gate, -jnp.inf)
    rank = jnp.zeros((rows, LANES), jnp.int32)
    for m in range(nb):
        gc = gm[:, m:m + 1]
        beats = (gc > gm) | ((gc == gm) & (m < blk))
        rank = rank + beats.astype(jnp.int32)
    sel = (rank < k_top) & (gm > -jnp.inf) & (gm < jnp.inf)
    bias = jnp.where(sel, 0.0, MASK_NEG)

    pad = jnp.zeros((LANES - ts, D_ATTN), F32)
    k_new = jnp.concatenate([kn, pad], axis=0).astype(BF16)
    v_new = jnp.concatenate([vn, pad], axis=0).astype(BF16)
    s_new = lax.dot_general(qs, k_new, NT_DIMS, preferred_element_type=F32)
    qi = lax.broadcasted_iota(jnp.int32, (rows, LANES), 0) % ts
    s_new = jnp.where(blk <= qi, s_new, MASK_NEG)

    mx = jnp.max(s_new, axis=-1, keepdims=True)
    for n in range(nb):
        s_list[n] = s_list[n] + bias[:, n:n + 1]
        mx = jnp.maximum(mx, jnp.max(s_list[n], axis=-1, keepdims=True))
    p_new = jnp.exp(s_new - mx)
    den = jnp.sum(p_new, axis=-1, keepdims=True)
    acc = jnp.dot(p_new.astype(BF16), v_new, preferred_element_type=F32)
    for n in range(nb):
        p_n = jnp.exp(s_list[n] - mx)
        den = den + jnp.sum(p_n, axis=-1, keepdims=True)
        vt = vt_ref[:, n * MOBA_BLOCK:(n + 1) * MOBA_BLOCK].astype(BF16)
        acc = acc + lax.dot_general(p_n.astype(BF16), vt, NT_DIMS, preferred_element_type=F32)
    acc = acc / den
    out = jnp.zeros((ts, D_ATTN), F32)
    for h in range(N_HEADS):
        out = out + jnp.where(lane_head == h, acc[h * ts:(h + 1) * ts, :], 0.0)
    return out


def _attn_sample(q, k_new, v_new, cache_kt, cache_vt, page_table, layer):
    b, ts, _ = q.shape
    n_pages = page_table.shape[1]
    page = cache_kt.shape[3]
    past_len = n_pages * page
    assert ts == SUBLANES and MOBA_BLOCK % page == 0 and past_len % MOBA_BLOCK == 0
    nb = past_len // MOBA_BLOCK
    assert nb <= LANES
    tok = pl.BlockSpec((1, ts, D_ATTN), lambda bi, pt: (bi, 0, 0))
    hbm = pl.BlockSpec(memory_space=pl.ANY)
    past_buf = pltpu.VMEM((2, D_ATTN, past_len), F32)
    return pl.pallas_call(
        functools.partial(_attn_sample_kernel, layer=layer, k_top=min(MOBA_TOP_K, nb)),
        grid_spec=pltpu.PrefetchScalarGridSpec(
            num_scalar_prefetch=1,
            grid=(b,),
            in_specs=[tok, tok, tok, hbm, hbm],
            out_specs=tok,
            scratch_shapes=[past_buf, past_buf, pltpu.SemaphoreType.DMA((2, 2))],
        ),
        out_shape=jax.ShapeDtypeStruct((b, ts, D_ATTN), F32),
        compiler_params=_params("arbitrary"),
    )(page_table, q, k_new, v_new, cache_kt, cache_vt)


def _lane_windows(shape):
    grp = lax.broadcasted_iota(jnp.int32, shape, len(shape) - 1) // (D_POOL // POOL_GROUPS)
    win = jnp.full(shape, POOL_WINDOWS[0], jnp.int32)
    for gi in range(1, POOL_GROUPS):
        win = jnp.where(grp == gi, POOL_WINDOWS[gi], win)
    return win


def _branch_prompt_kernel(ug_ref, vn_ref, xc_ref, prev_ref, ws_ref, bs_ref, wp_ref, ps_ref,
                          b_ref, c_ref):
    ti = pl.program_id(1)
    tm = ug_ref.shape[1]
    dg = D_GMLP // GMLP_GROUPS

    wr = lax.broadcasted_iota(jnp.int32, (GMLP_CHUNK, GMLP_GROUPS * GMLP_CHUNK), 0)
    wc = lax.broadcasted_iota(jnp.int32, (GMLP_CHUNK, GMLP_GROUPS * GMLP_CHUNK), 1) % GMLP_CHUNK
    w_tril = jnp.where(wc <= wr, ws_ref[...], 0.0).astype(BF16)
    lane_grp = lax.broadcasted_iota(jnp.int32, (GMLP_CHUNK, D_GMLP), 1) // dg
    for c in range(tm // GMLP_CHUNK):
        rows = slice(c * GMLP_CHUNK, (c + 1) * GMLP_CHUNK)
        vc = vn_ref[0, rows, :]
        v_stack = jnp.concatenate([jnp.where(lane_grp == gi, vc, 0.0) for gi in range(GMLP_GROUPS)],
                                  axis=0).astype(BF16)
        mixed = jnp.dot(w_tril, v_stack, preferred_element_type=F32) + bs_ref[...]
        b_ref[0, rows, :] = ug_ref[0, rows, :] * mixed

    x = xc_ref[0]
    prev = jnp.where(ti > 0, prev_ref[0], 0.0)
    xe = jnp.concatenate([prev, x], axis=0)
    s2 = xe + pltpu.roll(xe, 1, axis=0)
    s4 = s2 + pltpu.roll(s2, 2, axis=0)
    s8 = s4 + pltpu.roll(s4, 4, axis=0)
    s16 = s8 + pltpu.roll(s8, 8, axis=0)
    win = _lane_windows((tm, D_POOL))
    pre = 2 * SUBLANES
    wsum = jnp.where(win == 2, s2[pre:], jnp.where(win == 4, s4[pre:],
                     jnp.where(win == 8, s8[pre:], s16[pre:])))
    pos = ti * tm + lax.broadcasted_iota(jnp.int32, (tm, D_POOL), 0)
    count = jnp.minimum(win, pos + 1).astype(F32)
    pooled = wsum / count - x
    y = jnp.dot(pooled.astype(BF16), wp_ref[...], preferred_element_type=F32)
    c_ref[0] = y * ps_ref[...]


def _branch_prompt(ug, vn, xc, ws_cat, bs_t, wp_bd, ps, tm):
    b, t, _ = ug.shape
    assert t % tm == 0 and tm % GMLP_CHUNK == 0
    pre = 2 * SUBLANES
    tile = pl.BlockSpec((1, tm, D_GMLP), lambda bi, ti: (bi, ti, 0))
    prev = pl.BlockSpec((1, pre, D_POOL), lambda bi, ti: (bi, jnp.maximum(ti * (tm // pre) - 1, 0), 0))
    return pl.pallas_call(
        _branch_prompt_kernel,
        grid=(b, t // tm),
        in_specs=[tile, tile, tile, prev, _resident(ws_cat.shape), _resident(bs_t.shape),
                  _resident(wp_bd.shape), _resident(ps.shape)],
        out_specs=[tile, tile],
        out_shape=[jax.ShapeDtypeStruct((b, t, D_GMLP), F32), jax.ShapeDtypeStruct((b, t, D_POOL), F32)],
        compiler_params=_params("parallel", "parallel"),
    )(ug, vn, xc, xc, ws_cat, bs_t, wp_bd, ps)


def _branch_sample_kernel(ug_ref, vn_ref, xc_ref, pre_ref, wc_ref, bs_ref, wp_ref, ps_ref,
                          b_ref, c_ref, *, pos0):
    nb, ts, _ = ug_ref.shape
    vn = vn_ref[...]
    t_idx = lax.broadcasted_iota(jnp.int32, (ts, D_GMLP), 0)
    mixed = jnp.zeros((nb, ts, D_GMLP), F32) + bs_ref[...][None]
    for s in range(ts):
        coef = jnp.where(s <= t_idx, wc_ref[s], 0.0)
        mixed = mixed + coef[None] * vn[:, s:s + 1, :]
    b_ref[...] = ug_ref[...] * mixed

    x = xc_ref[...]
    pre = pre_ref[...]
    win = _lane_windows((ts, D_POOL))
    end = POOL_BUF + t_idx
    wsum = jnp.zeros((nb, ts, D_POOL), F32)
    for r in range(POOL_BUF + ts):
        row = pre[:, r:r + 1, :] if r < POOL_BUF else x[:, r - POOL_BUF:r - POOL_BUF + 1, :]
        inside = (r <= end) & (r > end - win)
        wsum = wsum + jnp.where(inside, 1.0, 0.0)[None] * row
    count = jnp.minimum(win, pos0 + t_idx + 1).astype(F32)
    pooled = wsum / count[None] - x
    y = jnp.dot(pooled.reshape(nb * ts, D_POOL).astype(BF16), wp_ref[...], preferred_element_type=F32)
    c_ref[...] = (y * ps_ref[...]).reshape(nb, ts, D_POOL)


def _branch_sample(ug, vn, xc, prefix, wc, bs_t, wp_bd, ps, pos0):
    b, ts, _ = ug.shape
    assert ts == SUBLANES
    full = lambda shape: pl.BlockSpec(shape, lambda i: (0,) * len(shape))
    return pl.pallas_call(
        functools.partial(_branch_sample_kernel, pos0=pos0),
        grid=(1,),
        in_specs=[full(ug.shape), full(vn.shape), full(xc.shape), full(prefix.shape), full(wc.shape),
                  full(bs_t.shape), full(wp_bd.shape), full(ps.shape)],
        out_specs=[full(ug.shape), full(xc.shape)],
        out_shape=[jax.ShapeDtypeStruct(ug.shape, F32), jax.ShapeDtypeStruct(xc.shape, F32)],
        compiler_params=_params("arbitrary"),
    )(ug, vn, xc, prefix, wc, bs_t, wp_bd, ps)


def _merge_mlp_kernel(h_ref, a_ref, b_ref, c_ref, gates_ref, wa_ref, wb_ref, wc_ref, wo_ref, g_ref,
                      wu_ref, wd_ref, gf_ref, o_ref, *, final, ff_chunk):
    def branch(x_ref, w_ref, j):
        y = jnp.dot(x_ref[...].astype(BF16), w_ref[...], preferred_element_type=F32)
        return gates_ref[:, j * D_MODEL:(j + 1) * D_MODEL].astype(F32) * y

    merged = branch(a_ref, wa_ref, 0) + branch(b_ref, wb_ref, 1) + branch(c_ref, wc_ref, 2)
    h1 = h_ref[...] + jnp.dot(merged.astype(BF16), wo_ref[...], preferred_element_type=F32)
    xb = _rms(h1, g_ref[...]).astype(BF16)
    acc = h1
    for c in range(D_FF // ff_chunk):
        cols = slice(c * ff_chunk, (c + 1) * ff_chunk)
        up = jnp.dot(xb, wu_ref[:, cols], preferred_element_type=F32)
        act = jnp.square(jnp.maximum(up, 0.0)).astype(BF16)
        acc = acc + jnp.dot(act, wd_ref[cols, :], preferred_element_type=F32)
    o_ref[...] = _rms(acc, gf_ref[...]) if final else acc


def _merge_mlp(h, a, b, c, gates, wa, wb, wc, wo, g, wu, wd, gf, layer, final, tm):
    t = h.shape[0]
    row = lambda w: pl.BlockSpec((tm, w), lambda i: (i, 0))
    stack = lambda w: _layer_resident(w.shape, layer)
    return pl.pallas_call(
        functools.partial(_merge_mlp_kernel, final=final, ff_chunk=1024),
        grid=(t // tm,),
        in_specs=[row(D_MODEL), row(D_ATTN), row(D_GMLP), row(D_POOL), row(3 * D_MODEL),
                  stack(wa), stack(wb), stack(wc), stack(wo), _resident((1, D_MODEL)),
                  stack(wu), stack(wd), _resident((1, D_MODEL))],
        out_specs=row(D_MODEL),
        out_shape=jax.ShapeDtypeStruct((t, D_MODEL), F32),
        compiler_params=_params("parallel"),
    )(h, a, b, c, gates, wa, wb, wc, wo, g, wu, wd, gf)


def _block_diag(w):
    g, d, e = w.shape
    eye = jnp.eye(g, dtype=w.dtype)
    return (eye[:, None, :, None] * w[:, :, None, :]).reshape(g * d, g * e)


def _pages_channel_major(cache):
    n_phys, depth, page, _, _ = cache.shape
    return jnp.transpose(cache, (0, 1, 3, 4, 2)).reshape(n_phys, depth, D_ATTN, page)


def _heads_last(kt):
    b, depth, _, t = kt.shape
    return jnp.transpose(kt.reshape(b, depth, N_HEADS, HEAD_DIM, t), (0, 1, 4, 2, 3))


def kernel(x_prompt, x_sample, cache_k, cache_v, state_pool, page_table, norm1_g, w_in, gmlp_norm_g,
           w_spatial, b_spatial, w_pool, pool_scale, w_branch_a, w_branch_b, w_branch_c, w_out,
           norm2_g, w_up, w_down, final_norm_g):
    bp, tp, _ = x_prompt.shape
    bs, ts, _ = x_sample.shape
    depth = w_in.shape[0]
    past_len = page_table.shape[1] * cache_k.shape[2]
    dg = D_GMLP // GMLP_GROUPS
    tm = min(512, tp)
    tms = min(512, bs * ts)

    ckt = _pages_channel_major(cache_k)
    cvt = _pages_channel_major(cache_v)
    w_in_b = w_in.astype(BF16)
    wkv_t = _kv_weights_transposed(w_in)
    wa_b, wb_b, wc_b = w_branch_a.astype(BF16), w_branch_b.astype(BF16), w_branch_c.astype(BF16)
    wo_b, wu_b, wd_b = w_out.astype(BF16), w_up.astype(BF16), w_down.astype(BF16)
    gf = final_norm_g.reshape(1, D_MODEL)

    hp = x_prompt.reshape(bp * tp, D_MODEL)
    hs = x_sample.reshape(bs * ts, D_MODEL)
    outs = {n: [] for n in ("pp", "ks", "vs", "ps", "gv")}
    kv_stacks = (jnp.zeros((bp, depth, D_ATTN, tp), F32), jnp.zeros((bp, depth, D_ATTN, tp), F32))
    for l in range(depth):
        g1 = norm1_g[l].reshape(1, D_MODEL)
        g2 = norm2_g[l].reshape(1, D_MODEL)
        gn = gmlp_norm_g[l].reshape(1, D_GMLP)
        ps = pool_scale[l].reshape(1, D_POOL)
        wp_bd = _block_diag(w_pool[l]).astype(BF16)
        ws_cat = jnp.transpose(w_spatial[l], (1, 0, 2)).reshape(GMLP_CHUNK, GMLP_GROUPS * GMLP_CHUNK)
        bs_full = jnp.repeat(b_spatial[l].T, dg, axis=1)
        wc_s = jnp.repeat(jnp.transpose(w_spatial[l][:, :ts, :ts], (2, 1, 0)), dg, axis=2)
        bs_s = bs_full[:ts]

        q, kst, vst, ug, vn, xc, gates = _proj(hp, g1, w_in_b, wkv_t, gn, l, bp, tm, kv_stacks)
        kv_stacks = (kst, vst)
        r3 = lambda a: a.reshape(bp, tp, a.shape[-1])
        a_out = _attn_prompt(r3(q), kst, vst, l)
        b_out, c_out = _branch_prompt(r3(ug), r3(vn), r3(xc), ws_cat, bs_full, wp_bd, ps, tm)
        hp = _merge_mlp(hp, a_out.reshape(bp * tp, D_ATTN), b_out.reshape(bp * tp, D_GMLP),
                        c_out.reshape(bp * tp, D_POOL), gates, wa_b, wb_b, wc_b, wo_b, g2, wu_b, wd_b,
                        gf, l, l == depth - 1, tm)
        outs["pp"].append(r3(xc)[:, tp - POOL_BUF:])

        q, k, v, ug, vn, xc, gates = _proj(hs, g1, w_in_b, wkv_t, gn, l, 1, tms)
        s3 = lambda a: a.reshape(bs, ts, a.shape[-1])
        a_out = _attn_sample(s3(q), s3(k), s3(v), ckt, cvt, page_table, l)
        prefix = state_pool[:, l]
        b_out, c_out = _branch_sample(s3(ug), s3(vn), s3(xc), prefix, wc_s, bs_s, wp_bd, ps, past_len)
        hs = _merge_mlp(hs, a_out.reshape(bs * ts, D_ATTN), b_out.reshape(bs * ts, D_GMLP),
                        c_out.reshape(bs * ts, D_POOL), gates, wa_b, wb_b, wc_b, wo_b, g2, wu_b, wd_b,
                        gf, l, l == depth - 1, tms)
        outs["ks"].append(k.reshape(bs, ts, N_HEADS, HEAD_DIM))
        outs["vs"].append(v.reshape(bs, ts, N_HEADS, HEAD_DIM))
        outs["ps"].append(jnp.concatenate([prefix, s3(xc)], axis=1)[:, -POOL_BUF:])
        outs["gv"].append(s3(vn))

    stack = lambda n: jnp.stack(outs[n], axis=1)
    return (hp.reshape(bp, tp, D_MODEL), hs.reshape(bs, ts, D_MODEL), _heads_last(kv_stacks[0]),
            _heads_last(kv_stacks[1]), stack("pp"), stack("ks"), stack("vs"), stack("ps"), stack("gv"))
```

```python
import functools

import jax
import jax.numpy as jnp
from jax import lax
from jax.experimental import pallas as pl
from jax.experimental.pallas import tpu as pltpu

F32 = jnp.float32
BF16 = jnp.bfloat16

D_MODEL = 1024
N_HEADS = 8
HEAD_DIM = 64
D_ATTN = N_HEADS * HEAD_DIM
MOBA_BLOCK = 256
MOBA_TOP_K = 3
GMLP_GROUPS = 4
D_GMLP = 256
GMLP_CHUNK = 128
POOL_GROUPS = 4
D_POOL = 256
POOL_WINDOWS = (2, 4, 8, 16)
POOL_BUF = max(POOL_WINDOWS) - 1
D_FF = 4 * D_MODEL
NORM_EPS = 1e-6
N_IN = 3 * D_ATTN + 2 * D_GMLP + D_POOL + 3 * D_MODEL
OFF_Q, OFF_K, OFF_V = 0, D_ATTN, 2 * D_ATTN
OFF_U = 3 * D_ATTN
OFF_VG = OFF_U + D_GMLP
OFF_XC = OFF_VG + D_GMLP
OFF_GATES = OFF_XC + D_POOL

LANES = 128
SUBLANES = 8
HEADS_PER_TILE = LANES // HEAD_DIM
ATTN_SCALE = HEAD_DIM ** -0.5
LOG2E = 1.4426950408889634
MASK_NEG = -(2.0 ** 100)
VMEM_LIMIT = 56 * 1024 * 1024
POOL_PREV_ROWS = 2 * SUBLANES

NT_DIMS = (((1,), (1,)), ((), ()))


def _params(*sem):
    return pltpu.CompilerParams(dimension_semantics=sem, vmem_limit_bytes=VMEM_LIMIT)


def _resident(shape):
    nd = len(shape)
    return pl.BlockSpec(shape, lambda *_: (0,) * nd, pipeline_mode=pl.Buffered(1))


def _layer_resident(stacked_shape, layer):
    nd = len(stacked_shape) - 1
    return pl.BlockSpec((None,) + tuple(stacked_shape[1:]), lambda *_: (layer,) + (0,) * nd,
                        pipeline_mode=pl.Buffered(1))


def _split_bf16(x):
    hi = x.astype(BF16)
    return hi, (x - hi.astype(F32)).astype(BF16)


def _rms(x, g):
    return x * lax.rsqrt(jnp.mean(x * x, axis=-1, keepdims=True) + NORM_EPS) * g


def _proj_kernel(h_ref, g_ref, w_ref, wkv_t_ref, gn_ref, *refs, kv_transposed):
    q_ref, k_ref, v_ref, ug_ref, vn_ref, xc_ref, gates_ref = refs[-7:]
    xb = _rms(h_ref[...], g_ref[...]).astype(BF16)

    def seg(lo, width):
        return jnp.dot(xb, w_ref[:, lo:lo + width], preferred_element_type=F32)

    q_ref[...] = seg(OFF_Q, D_ATTN)
    if kv_transposed:
        k_ref[0, 0] = lax.dot_general(wkv_t_ref[0:D_ATTN, :], xb, NT_DIMS,
                                      preferred_element_type=F32)
        v_ref[0, 0] = lax.dot_general(wkv_t_ref[D_ATTN:2 * D_ATTN, :], xb, NT_DIMS,
                                      preferred_element_type=F32)
    else:
        k_ref[...] = seg(OFF_K, D_ATTN)
        v_ref[...] = seg(OFF_V, D_ATTN)
    ug_ref[...] = jax.nn.gelu(seg(OFF_U, D_GMLP))
    gv = jax.nn.gelu(seg(OFF_VG, D_GMLP))
    gc = gv - jnp.mean(gv, axis=-1, keepdims=True)
    vn_ref[...] = gc * lax.rsqrt(jnp.mean(gc * gc, axis=-1, keepdims=True) + NORM_EPS) * gn_ref[...]
    xc_ref[...] = seg(OFF_XC, D_POOL)
    for j in range(3):
        z = seg(OFF_GATES + j * D_MODEL, D_MODEL)
        gates_ref[:, j * D_MODEL:(j + 1) * D_MODEL] = jax.nn.sigmoid(z).astype(BF16)


def _proj(h, g, w_in, wkv_t, gn, layer, nseq, tm, kv_stacks=None):
    n = h.shape[0]
    t = n // nseq
    nt = t // tm
    row = lambda w: pl.BlockSpec((tm, w), lambda bi, ti: (bi * nt + ti, 0))
    flat = lambda w, dt=F32: jax.ShapeDtypeStruct((n, w), dt)
    in_specs = [row(D_MODEL), _resident((1, D_MODEL)), _layer_resident(w_in.shape, layer),
                _layer_resident(wkv_t.shape, layer), _resident((1, D_GMLP))]
    operands = [h, g, w_in, wkv_t, gn]
    if kv_stacks is not None:
        kv_spec = pl.BlockSpec((1, 1, D_ATTN, tm), lambda bi, ti: (bi, layer, 0, ti))
        kv_shape = jax.ShapeDtypeStruct(kv_stacks[0].shape, F32)
        aliases = {len(operands): 1, len(operands) + 1: 2}
        in_specs += [pl.BlockSpec(memory_space=pl.ANY)] * 2
        operands += list(kv_stacks)
    else:
        kv_spec, kv_shape, aliases = row(D_ATTN), flat(D_ATTN), {}
    return pl.pallas_call(
        functools.partial(_proj_kernel, kv_transposed=kv_stacks is not None),
        grid=(nseq, nt),
        in_specs=in_specs,
        out_specs=[row(D_ATTN), kv_spec, kv_spec, row(D_GMLP), row(D_GMLP), row(D_POOL),
                   row(3 * D_MODEL)],
        out_shape=[flat(D_ATTN), kv_shape, kv_shape, flat(D_GMLP), flat(D_GMLP), flat(D_POOL),
                   flat(3 * D_MODEL, BF16)],
        input_output_aliases=aliases,
        compiler_params=_params("parallel", "parallel"),
    )(*operands)


def _transpose_cast_kernel(w_ref, o_ref):
    o_ref[0] = w_ref[0].T.astype(BF16)


def _kv_weights_transposed(w_in):
    depth = w_in.shape[0]
    assert OFF_K % D_ATTN == 0 and OFF_V == OFF_K + D_ATTN
    return pl.pallas_call(
        _transpose_cast_kernel,
        grid=(depth, 2),
        in_specs=[pl.BlockSpec((1, D_MODEL, D_ATTN), lambda l, j: (l, 0, OFF_K // D_ATTN + j))],
        out_specs=pl.BlockSpec((1, D_ATTN, D_MODEL), lambda l, j: (l, j, 0)),
        out_shape=jax.ShapeDtypeStruct((depth, 2 * D_ATTN, D_MODEL), BF16),
        compiler_params=_params("parallel", "parallel"),
    )(w_in)


def _attn_prompt_kernel(q_ref, kt_ref, vt_ref, o_ref, kaug_ref, vb_ref, kbd_ref, qaug_ref, s_ref,
                        mx_ref, l_ref, acc_ref, *, k_top):
    i = pl.program_id(1)
    nb, n_pairs = kaug_ref.shape[0], kaug_ref.shape[1]
    nbp = SUBLANES
    fold = lambda x: (x[:, 0:LANES], x[:, LANES:2 * LANES])

    @pl.when(i == 0)
    def _():
        lane = lax.broadcasted_iota(jnp.int32, (LANES, LANES), 1)
        oh_row = lax.broadcasted_iota(jnp.int32, (LANES, MOBA_BLOCK), 0)
        half = lax.broadcasted_iota(jnp.int32, (nbp, LANES), 1) // HEAD_DIM
        kbd_ref[...] = jnp.zeros(kbd_ref.shape, BF16)
        for p in range(n_pairs):
            chans = slice(p * LANES, (p + 1) * LANES)
            km_t = jnp.zeros((LANES, LANES), F32)
            for n in range(nb):
                cols = slice(n * MOBA_BLOCK, (n + 1) * MOBA_BLOCK)
                ktn = kt_ref[0, chans, cols]
                kaug_ref[n, p, 0:LANES, :] = ktn.astype(BF16)
                onehot = (oh_row % nbp == n) & (oh_row // (HEADS_PER_TILE * nbp) == p)
                kaug_ref[n, p, LANES:2 * LANES, :] = jnp.where(onehot, 1.0, 0.0).astype(BF16)
                vb_ref[n, p] = vt_ref[0, chans, cols].astype(BF16)
                col = jnp.sum(ktn, axis=1, keepdims=True) * (1.0 / MOBA_BLOCK)
                km_t = jnp.where(lane == n, col, km_t)
            km = km_t.T
            km = jnp.concatenate([jnp.where(half == hh, km[0:nbp], 0.0)
                                  for hh in range(HEADS_PER_TILE)], axis=0)
            rows = slice(p * HEADS_PER_TILE * nbp, (p + 1) * HEADS_PER_TILE * nbp)
            km_hi, km_lo = _split_bf16(km)
            kbd_ref[0, rows, chans] = km_hi
            kbd_ref[1, rows, chans] = km_lo

    qf = q_ref[0]
    q_hi, q_lo = _split_bf16(qf)
    nt_dot = lambda a, b: lax.dot_general(a, b, NT_DIMS, preferred_element_type=F32)
    g = nt_dot(kbd_ref[0], q_hi) + (nt_dot(kbd_ref[0], q_lo) + nt_dot(kbd_ref[1], q_hi))
    gm = g.reshape(N_HEADS, nbp, MOBA_BLOCK)
    blk = lax.broadcasted_iota(jnp.int32, gm.shape, 1)
    past = blk < i
    gm = jnp.where(past, gm, -jnp.inf)
    rank = jnp.zeros(gm.shape, jnp.int32)
    for m in range(nb):
        gr = gm[:, m:m + 1, :]
        beats = (gr > gm) | ((gr == gm) & (m < blk))
        rank = rank + beats.astype(jnp.int32)
    sel = past & (rank < k_top) & (gm > -jnp.inf) & (gm < jnp.inf)
    sel_t = jnp.where(sel, 1.0, 0.0).reshape(N_HEADS * nbp, MOBA_BLOCK)
    sel_t = jnp.concatenate([sel_t, jnp.zeros((LANES - N_HEADS * nbp, MOBA_BLOCK), F32)], axis=0)
    rr = lax.broadcasted_iota(jnp.int32, (MOBA_BLOCK, MOBA_BLOCK), 0)
    cc = lax.broadcasted_iota(jnp.int32, (MOBA_BLOCK, MOBA_BLOCK), 1)
    eye = jnp.where(rr == cc, 1.0, 0.0).astype(BF16)
    sel_n = lax.dot_general(eye, sel_t.astype(BF16), NT_DIMS, preferred_element_type=F32)
    bias_all = jnp.where(sel_n > 0.5, 0.0, MASK_NEG)
    lane = lax.broadcasted_iota(jnp.int32, (MOBA_BLOCK, LANES), 1)
    for h in range(N_HEADS):
        p, hh = divmod(h, HEADS_PER_TILE)
        qh = jnp.where(lane // HEAD_DIM == hh, qf[:, p * LANES:(p + 1) * LANES], 0.0)
        qaug_ref[h, :, 0:LANES] = (qh * (ATTN_SCALE * LOG2E)).astype(BF16)
        qaug_ref[h, :, LANES:2 * LANES] = jnp.where(lane // nbp == h, bias_all, 0.0).astype(BF16)

    mx_ref[...] = jnp.full(mx_ref.shape, MASK_NEG, F32)

    def over_past_blocks(block_fn):
        def pair(j, carry):
            block_fn(2 * j)
            block_fn(2 * j + 1)
            return carry

        lax.fori_loop(0, i // 2, pair, 0)

        @pl.when(i % 2 == 1)
        def _():
            block_fn(i - 1)

    def scores(n):
        for h in range(N_HEADS):
            s = jnp.dot(qaug_ref[h], kaug_ref[n, h // HEADS_PER_TILE],
                        preferred_element_type=F32)
            s_ref[h, n] = s
            lo, hi = fold(s)
            mx_ref[h] = jnp.maximum(mx_ref[h], jnp.maximum(lo, hi))

    over_past_blocks(scores)

    causal = cc <= rr
    for h in range(N_HEADS):
        p = h // HEADS_PER_TILE
        s_own = jnp.dot(qaug_ref[h, :, 0:LANES], kaug_ref[i, p, 0:LANES, :],
                        preferred_element_type=F32)
        s_own = jnp.where(causal, s_own, MASK_NEG)
        lo, hi = fold(s_own)
        mx = jnp.max(jnp.maximum(mx_ref[h], jnp.maximum(lo, hi)), axis=-1, keepdims=True)
        mx_ref[h] = jnp.broadcast_to(mx, (MOBA_BLOCK, LANES))
        p_own = jnp.exp2(s_own - mx)
        lo, hi = fold(p_own)
        l_ref[h] = lo + hi
        acc_ref[h] = lax.dot_general(p_own.astype(BF16), vb_ref[i, p], NT_DIMS,
                                     preferred_element_type=F32)

    def values(n):
        for h in range(N_HEADS):
            mx = mx_ref[h]
            lo, hi = fold(s_ref[h, n])
            p_lo = jnp.exp2(lo - mx)
            p_hi = jnp.exp2(hi - mx)
            l_ref[h] += p_lo + p_hi
            pb = jnp.concatenate([p_lo, p_hi], axis=1).astype(BF16)
            acc_ref[h] += lax.dot_general(pb, vb_ref[n, h // HEADS_PER_TILE], NT_DIMS,
                                          preferred_element_type=F32)

    over_past_blocks(values)

    for p in range(n_pairs):
        o = [acc_ref[p * HEADS_PER_TILE + hh]
             / jnp.sum(l_ref[p * HEADS_PER_TILE + hh], axis=-1, keepdims=True)
             for hh in range(HEADS_PER_TILE)]
        o_ref[0, :, p * LANES:(p + 1) * LANES] = jnp.where(lane < HEAD_DIM, o[0], o[1])


def _attn_prompt(q, kt, vt, layer):
    b, t, _ = q.shape
    assert t % MOBA_BLOCK == 0
    nb = t // MOBA_BLOCK
    n_pairs = D_ATTN // LANES
    assert nb <= SUBLANES and N_HEADS * SUBLANES <= LANES
    qspec = pl.BlockSpec((1, MOBA_BLOCK, D_ATTN), lambda bi, i: (bi, i, 0))
    kspec = pl.BlockSpec((1, None, D_ATTN, t), lambda bi, i: (bi, layer, 0, 0))
    head_acc = pltpu.VMEM((N_HEADS, MOBA_BLOCK, LANES), F32)
    return pl.pallas_call(
        functools.partial(_attn_prompt_kernel, k_top=min(MOBA_TOP_K, nb)),
        grid=(b, nb),
        in_specs=[qspec, kspec, kspec],
        out_specs=qspec,
        out_shape=jax.ShapeDtypeStruct((b, t, D_ATTN), F32),
        scratch_shapes=[pltpu.VMEM((nb, n_pairs, 2 * LANES, MOBA_BLOCK), BF16),
                        pltpu.VMEM((nb, n_pairs, LANES, MOBA_BLOCK), BF16),
                        pltpu.VMEM((2, N_HEADS * SUBLANES, D_ATTN), BF16),
                        pltpu.VMEM((N_HEADS, MOBA_BLOCK, 2 * LANES), BF16),
                        pltpu.VMEM((N_HEADS, nb, MOBA_BLOCK, MOBA_BLOCK), F32),
                        head_acc, head_acc, head_acc],
        compiler_params=_params("parallel", "arbitrary"),
    )(q, kt, vt)


def _attn_sample_kernel(pt_ref, q_ref, kn_ref, vn_ref, ck_hbm, cv_hbm, o_ref, kbuf, vbuf, sem, *,
                        layer, k_top):
    b = pl.program_id(0)
    n_pages = pt_ref.shape[1]

    def page_copies(sample, slot):
        copies = []
        for j in range(n_pages):
            pid = pt_ref[sample, j]
            copies.append(pltpu.make_async_copy(ck_hbm.at[pid, layer], kbuf.at[slot, j],
                                                sem.at[0, slot]))
            copies.append(pltpu.make_async_copy(cv_hbm.at[pid, layer], vbuf.at[slot, j],
                                                sem.at[1, slot]))
        return copies

    slot = b % 2

    @pl.when(b == 0)
    def _():
        for c in page_copies(0, 0):
            c.start()

    @pl.when(b + 1 < pl.num_programs(0))
    def _():
        for c in page_copies(b + 1, 1 - slot):
            c.start()

    for c in page_copies(b, slot):
        c.wait()
    o_ref[0] = _attend_sample(q_ref[0], kn_ref[0], vn_ref[0], kbuf.at[slot], vbuf.at[slot], k_top)


def _attend_sample(q, kn, vn, kt_ref, vt_ref, k_top):
    ts = q.shape[0]
    ppb = MOBA_BLOCK // kt_ref.shape[2]
    nb = kt_ref.shape[0] // ppb
    rows = N_HEADS * ts
    block = lambda ref, n: jnp.concatenate([ref[n * ppb + j] for j in range(ppb)], axis=1)

    lane_head = lax.broadcasted_iota(jnp.int32, (ts, D_ATTN), 1) // HEAD_DIM
    q_bd = jnp.concatenate([jnp.where(lane_head == h, q, 0.0) for h in range(N_HEADS)], axis=0)
    qs = (q_bd * ATTN_SCALE).astype(BF16)

    lane = lax.broadcasted_iota(jnp.int32, (D_ATTN, LANES), 1)
    km_t = jnp.zeros((D_ATTN, LANES), F32)
    s_list = []
    for n in range(nb):
        kt = block(kt_ref, n)
        col = jnp.sum(kt, axis=1, keepdims=True) * (1.0 / MOBA_BLOCK)
        km_t = jnp.where(lane == n, col, km_t)
        s_list.append(jnp.dot(qs, kt.astype(BF16), preferred_element_type=F32))

    gate = jnp.dot(q_bd, km_t, precision=lax.Precision.HIGHEST, preferred_element_type=F32)
    blk = lax.broadcasted_iota(jnp.int32, (rows, LANES), 1)
    gm = jnp.where(blk < nb, gate, -jnp.inf)
    rank = jnp.zeros((rows, LANES), jnp.int32)
    for m in range(nb):
        gc = gm[:, m:m + 1]
        beats = (gc > gm) | ((gc == gm) & (m < blk))
        rank = rank + beats.astype(jnp.int32)
    sel = (rank < k_top) & (gm > -jnp.inf) & (gm < jnp.inf)
    bias = jnp.where(sel, 0.0, MASK_NEG)

    pad = jnp.zeros((LANES - ts, D_ATTN), F32)
    k_new = jnp.concatenate([kn, pad], axis=0).astype(BF16)
    v_new = jnp.concatenate([vn, pad], axis=0).astype(BF16)
    s_new = lax.dot_general(qs, k_new, NT_DIMS, preferred_element_type=F32)
    qi = lax.broadcasted_iota(jnp.int32, (rows, LANES), 0) % ts
    s_new = jnp.where(blk <= qi, s_new, MASK_NEG)

    mx = jnp.max(s_new, axis=-1, keepdims=True)
    for n in range(nb):
        s_list[n] = s_list[n] + bias[:, n:n + 1]
        mx = jnp.maximum(mx, jnp.max(s_list[n], axis=-1, keepdims=True))
    p_new = jnp.exp(s_new - mx)
    den = jnp.sum(p_new, axis=-1, keepdims=True)
    acc = jnp.dot(p_new.astype(BF16), v_new, preferred_element_type=F32)
    for n in range(nb):
        p_n = jnp.exp(s_list[n] - mx)
        den = den + jnp.sum(p_n, axis=-1, keepdims=True)
        vt = block(vt_ref, n).astype(BF16)
        acc = acc + lax.dot_general(p_n.astype(BF16), vt, NT_DIMS, preferred_element_type=F32)
    acc = acc / den
    out = jnp.zeros((ts, D_ATTN), F32)
    for h in range(N_HEADS):
        out = out + jnp.where(lane_head == h, acc[h * ts:(h + 1) * ts, :], 0.0)
    return out


def _attn_sample(q, k_new, v_new, cache_kt, cache_vt, page_table, layer):
    b, ts, _ = q.shape
    n_pages = page_table.shape[1]
    page = cache_kt.shape[3]
    past_len = n_pages * page
    assert ts == SUBLANES and MOBA_BLOCK % page == 0 and past_len % MOBA_BLOCK == 0
    nb = past_len // MOBA_BLOCK
    assert nb <= LANES
    tok = pl.BlockSpec((1, ts, D_ATTN), lambda bi, pt: (bi, 0, 0))
    hbm = pl.BlockSpec(memory_space=pl.ANY)
    past_buf = pltpu.VMEM((2, n_pages, D_ATTN, page), F32)
    return pl.pallas_call(
        functools.partial(_attn_sample_kernel, layer=layer, k_top=min(MOBA_TOP_K, nb)),
        grid_spec=pltpu.PrefetchScalarGridSpec(
            num_scalar_prefetch=1,
            grid=(b,),
            in_specs=[tok, tok, tok, hbm, hbm],
            out_specs=tok,
            scratch_shapes=[past_buf, past_buf, pltpu.SemaphoreType.DMA((2, 2))],
        ),
        out_shape=jax.ShapeDtypeStruct((b, ts, D_ATTN), F32),
        compiler_params=_params("arbitrary"),
    )(page_table, q, k_new, v_new, cache_kt, cache_vt)


def _lane_windows(shape):
    grp = lax.broadcasted_iota(jnp.int32, shape, len(shape) - 1) // (D_POOL // POOL_GROUPS)
    win = jnp.full(shape, POOL_WINDOWS[0], jnp.int32)
    for gi in range(1, POOL_GROUPS):
        win = jnp.where(grp == gi, POOL_WINDOWS[gi], win)
    return win


def _prompt_branches(ti, ug_ref, vn_ref, xc_ref, prev_ref, ws_ref, bs_ref, wp_ref, ps_ref):
    tm = ug_ref.shape[0]
    dg = D_GMLP // GMLP_GROUPS

    wr = lax.broadcasted_iota(jnp.int32, (GMLP_CHUNK, GMLP_GROUPS * GMLP_CHUNK), 0)
    wc = lax.broadcasted_iota(jnp.int32, (GMLP_CHUNK, GMLP_GROUPS * GMLP_CHUNK), 1) % GMLP_CHUNK
    w_tril = jnp.where(wc <= wr, ws_ref[...], 0.0).astype(BF16)
    lane_grp = lax.broadcasted_iota(jnp.int32, (GMLP_CHUNK, D_GMLP), 1) // dg
    b_chunks = []
    for c in range(tm // GMLP_CHUNK):
        rows = slice(c * GMLP_CHUNK, (c + 1) * GMLP_CHUNK)
        vc = vn_ref[rows, :]
        v_stack = jnp.concatenate([jnp.where(lane_grp == gi, vc, 0.0) for gi in range(GMLP_GROUPS)],
                                  axis=0).astype(BF16)
        mixed = jnp.dot(w_tril, v_stack, preferred_element_type=F32) + bs_ref[...]
        b_chunks.append(ug_ref[rows, :] * mixed)
    b_out = jnp.concatenate(b_chunks, axis=0)

    x = xc_ref[...]
    prev = jnp.where(ti > 0, prev_ref[...], 0.0)
    xe = jnp.concatenate([prev, x], axis=0)
    s2 = xe + pltpu.roll(xe, 1, axis=0)
    s4 = s2 + pltpu.roll(s2, 2, axis=0)
    s8 = s4 + pltpu.roll(s4, 4, axis=0)
    s16 = s8 + pltpu.roll(s8, 8, axis=0)
    win = _lane_windows((tm, D_POOL))
    pre = POOL_PREV_ROWS
    wsum = jnp.where(win == 2, s2[pre:], jnp.where(win == 4, s4[pre:],
                     jnp.where(win == 8, s8[pre:], s16[pre:])))
    pos = ti * tm + lax.broadcasted_iota(jnp.int32, (tm, D_POOL), 0)
    count = jnp.minimum(win, pos + 1).astype(F32)
    pooled = wsum / count - x
    y = jnp.dot(pooled.astype(BF16), wp_ref[...], preferred_element_type=F32)
    return b_out, y * ps_ref[...]


def _branch_sample_kernel(ug_ref, vn_ref, xc_ref, pre_ref, wc_ref, bs_ref, wp_ref, ps_ref,
                          b_ref, c_ref, *, pos0):
    nb, ts, _ = ug_ref.shape
    vn = vn_ref[...]
    t_idx = lax.broadcasted_iota(jnp.int32, (ts, D_GMLP), 0)
    mixed = jnp.zeros((nb, ts, D_GMLP), F32) + bs_ref[...][None]
    for s in range(ts):
        coef = jnp.where(s <= t_idx, wc_ref[s], 0.0)
        mixed = mixed + coef[None] * vn[:, s:s + 1, :]
    b_ref[...] = ug_ref[...] * mixed

    x = xc_ref[...]
    pre = pre_ref[...]
    win = _lane_windows((ts, D_POOL))
    end = POOL_BUF + t_idx
    wsum = jnp.zeros((nb, ts, D_POOL), F32)
    for r in range(POOL_BUF + ts):
        row = pre[:, r:r + 1, :] if r < POOL_BUF else x[:, r - POOL_BUF:r - POOL_BUF + 1, :]
        inside = (r <= end) & (r > end - win)
        wsum = wsum + jnp.where(inside, 1.0, 0.0)[None] * row
    count = jnp.minimum(win, pos0 + t_idx + 1).astype(F32)
    pooled = wsum / count[None] - x
    y = jnp.dot(pooled.reshape(nb * ts, D_POOL).astype(BF16), wp_ref[...], preferred_element_type=F32)
    c_ref[...] = (y * ps_ref[...]).reshape(nb, ts, D_POOL)


def _branch_sample(ug, vn, xc, prefix, wc, bs_t, wp_bd, ps, pos0):
    b, ts, _ = ug.shape
    assert ts == SUBLANES
    full = lambda shape: pl.BlockSpec(shape, lambda i: (0,) * len(shape))
    return pl.pallas_call(
        functools.partial(_branch_sample_kernel, pos0=pos0),
        grid=(1,),
        in_specs=[full(ug.shape), full(vn.shape), full(xc.shape), full(prefix.shape), full(wc.shape),
                  full(bs_t.shape), full(wp_bd.shape), full(ps.shape)],
        out_specs=[full(ug.shape), full(xc.shape)],
        out_shape=[jax.ShapeDtypeStruct(ug.shape, F32), jax.ShapeDtypeStruct(xc.shape, F32)],
        compiler_params=_params("arbitrary"),
    )(ug, vn, xc, prefix, wc, bs_t, wp_bd, ps)


def _merge_mlp_kernel(h_ref, a_ref, *refs, final, ff_chunk, n_branch_refs):
    branch_refs = refs[:n_branch_refs]
    gates_ref, wa_ref, wb_ref, wc_ref, wo_ref, g_ref, wu_ref, wd_ref, gf_ref, o_ref = refs[n_branch_refs:]
    if n_branch_refs == 2:
        b_out, c_out = branch_refs[0][...], branch_refs[1][...]
    else:
        b_out, c_out = _prompt_branches(pl.program_id(1), *branch_refs)

    def branch(x, w_ref, j):
        y = jnp.dot(x.astype(BF16), w_ref[...], preferred_element_type=F32)
        return gates_ref[:, j * D_MODEL:(j + 1) * D_MODEL].astype(F32) * y

    merged = branch(a_ref[...], wa_ref, 0) + branch(b_out, wb_ref, 1) + branch(c_out, wc_ref, 2)
    h1 = h_ref[...] + jnp.dot(merged.astype(BF16), wo_ref[...], preferred_element_type=F32)
    xb = _rms(h1, g_ref[...]).astype(BF16)
    acc = h1
    for c in range(D_FF // ff_chunk):
        cols = slice(c * ff_chunk, (c + 1) * ff_chunk)
        up = jnp.dot(xb, wu_ref[:, cols], preferred_element_type=F32)
        act = jnp.square(jnp.maximum(up, 0.0)).astype(BF16)
        acc = acc + jnp.dot(act, wd_ref[cols, :], preferred_element_type=F32)
    o_ref[...] = _rms(acc, gf_ref[...]) if final else acc


def _merge_mlp(h, a, branches, gates, wa, wb, wc, wo, g, wu, wd, gf, layer, final, nseq, tm):
    n = h.shape[0]
    nt = n // nseq // tm
    row = lambda w: pl.BlockSpec((tm, w), lambda bi, ti: (bi * nt + ti, 0))
    stack = lambda w: _layer_resident(w.shape, layer)
    if len(branches) == 2:
        branch_specs, branch_ops = [row(D_GMLP), row(D_POOL)], list(branches)
    else:
        ug, vn, xc, ws_cat, bs_t, wp_bd, ps = branches
        assert tm % GMLP_CHUNK == 0 and tm % POOL_PREV_ROWS == 0
        per = tm // POOL_PREV_ROWS
        prev = pl.BlockSpec((POOL_PREV_ROWS, D_POOL),
                            lambda bi, ti: (jnp.maximum((bi * nt + ti) * per - 1, 0), 0))
        branch_specs = [row(D_GMLP), row(D_GMLP), row(D_POOL), prev] + [
            _resident(w.shape) for w in (ws_cat, bs_t, wp_bd, ps)]
        branch_ops = [ug, vn, xc, xc, ws_cat, bs_t, wp_bd, ps]
    return pl.pallas_call(
        functools.partial(_merge_mlp_kernel, final=final, ff_chunk=1024,
                          n_branch_refs=len(branch_ops)),
        grid=(nseq, nt),
        in_specs=[row(D_MODEL), row(D_ATTN)] + branch_specs + [
            row(3 * D_MODEL), stack(wa), stack(wb), stack(wc), stack(wo), _resident((1, D_MODEL)),
            stack(wu), stack(wd), _resident((1, D_MODEL))],
        out_specs=row(D_MODEL),
        out_shape=jax.ShapeDtypeStruct((n, D_MODEL), F32),
        compiler_params=_params("parallel", "parallel"),
    )(h, a, *branch_ops, gates, wa, wb, wc, wo, g, wu, wd, gf)


def _block_diag(w):
    g, d, e = w.shape
    eye = jnp.eye(g, dtype=w.dtype)
    return (eye[:, None, :, None] * w[:, :, None, :]).reshape(g * d, g * e)


def _pages_channel_major(cache):
    n_phys, depth, page, _, _ = cache.shape
    return jnp.transpose(cache, (0, 1, 3, 4, 2)).reshape(n_phys, depth, D_ATTN, page)


def _heads_last(kt):
    b, depth, _, t = kt.shape
    return jnp.transpose(kt.reshape(b, depth, N_HEADS, HEAD_DIM, t), (0, 1, 4, 2, 3))


def kernel(x_prompt, x_sample, cache_k, cache_v, state_pool, page_table, norm1_g, w_in, gmlp_norm_g,
           w_spatial, b_spatial, w_pool, pool_scale, w_branch_a, w_branch_b, w_branch_c, w_out,
           norm2_g, w_up, w_down, final_norm_g):
    bp, tp, _ = x_prompt.shape
    bs, ts, _ = x_sample.shape
    depth = w_in.shape[0]
    past_len = page_table.shape[1] * cache_k.shape[2]
    dg = D_GMLP // GMLP_GROUPS
    tm = min(512, tp)
    tms = min(512, bs * ts)

    ckt = _pages_channel_major(cache_k)
    cvt = _pages_channel_major(cache_v)
    w_in_b = w_in.astype(BF16)
    wkv_t = _kv_weights_transposed(w_in)
    wa_b, wb_b, wc_b = w_branch_a.astype(BF16), w_branch_b.astype(BF16), w_branch_c.astype(BF16)
    wo_b, wu_b, wd_b = w_out.astype(BF16), w_up.astype(BF16), w_down.astype(BF16)
    gf = final_norm_g.reshape(1, D_MODEL)

    hp = x_prompt.reshape(bp * tp, D_MODEL)
    hs = x_sample.reshape(bs * ts, D_MODEL)
    outs = {n: [] for n in ("pp", "ks", "vs", "ps", "gv")}
    kv_stacks = (jnp.zeros((bp, depth, D_ATTN, tp), F32), jnp.zeros((bp, depth, D_ATTN, tp), F32))
    for l in range(depth):
        g1 = norm1_g[l].reshape(1, D_MODEL)
        g2 = norm2_g[l].reshape(1, D_MODEL)
        gn = gmlp_norm_g[l].reshape(1, D_GMLP)
        ps = pool_scale[l].reshape(1, D_POOL)
        wp_bd = _block_diag(w_pool[l]).astype(BF16)
        ws_cat = jnp.transpose(w_spatial[l], (1, 0, 2)).reshape(GMLP_CHUNK, GMLP_GROUPS * GMLP_CHUNK)
        bs_full = jnp.repeat(b_spatial[l].T, dg, axis=1)
        wc_s = jnp.repeat(jnp.transpose(w_spatial[l][:, :ts, :ts], (2, 1, 0)), dg, axis=2)
        bs_s = bs_full[:ts]

        q, kst, vst, ug, vn, xc, gates = _proj(hp, g1, w_in_b, wkv_t, gn, l, bp, tm, kv_stacks)
        kv_stacks = (kst, vst)
        r3 = lambda a: a.reshape(bp, tp, a.shape[-1])
        a_out = _attn_prompt(r3(q), kst, vst, l)
        hp = _merge_mlp(hp, a_out.reshape(bp * tp, D_ATTN), (ug, vn, xc, ws_cat, bs_full, wp_bd, ps),
                        gates, wa_b, wb_b, wc_b, wo_b, g2, wu_b, wd_b, gf, l, l == depth - 1, bp, tm)
        outs["pp"].append(r3(xc)[:, tp - POOL_BUF:])

        q, k, v, ug, vn, xc, gates = _proj(hs, g1, w_in_b, wkv_t, gn, l, 1, tms)
        s3 = lambda a: a.reshape(bs, ts, a.shape[-1])
        a_out = _attn_sample(s3(q), s3(k), s3(v), ckt, cvt, page_table, l)
        prefix = state_pool[:, l]
        b_out, c_out = _branch_sample(s3(ug), s3(vn), s3(xc), prefix, wc_s, bs_s, wp_bd, ps, past_len)
        hs = _merge_mlp(hs, a_out.reshape(bs * ts, D_ATTN),
                        (b_out.reshape(bs * ts, D_GMLP), c_out.reshape(bs * ts, D_POOL)),
                        gates, wa_b, wb_b, wc_b, wo_b, g2, wu_b, wd_b, gf, l, l == depth - 1, 1, tms)
        outs["ks"].append(k.reshape(bs, ts, N_HEADS, HEAD_DIM))
        outs["vs"].append(v.reshape(bs, ts, N_HEADS, HEAD_DIM))
        outs["ps"].append(jnp.concatenate([prefix, s3(xc)], axis=1)[:, -POOL_BUF:])
        outs["gv"].append(s3(vn))

    stack = lambda n: jnp.stack(outs[n], axis=1)
    return (hp.reshape(bp, tp, D_MODEL), hs.reshape(bs, ts, D_MODEL), _heads_last(kv_stacks[0]),
            _heads_last(kv_stacks[1]), stack("pp"), stack("ks"), stack("vs"), stack("ps"), stack("gv"))
```

```python
import functools

import jax
import jax.numpy as jnp
from jax import lax
from jax.experimental import pallas as pl
from jax.experimental.pallas import tpu as pltpu

F32 = jnp.float32
BF16 = jnp.bfloat16

D_MODEL = 1024
N_HEADS = 8
HEAD_DIM = 64
D_ATTN = N_HEADS * HEAD_DIM
MOBA_BLOCK = 256
MOBA_TOP_K = 3
GMLP_GROUPS = 4
D_GMLP = 256
GMLP_CHUNK = 128
POOL_GROUPS = 4
D_POOL = 256
POOL_WINDOWS = (2, 4, 8, 16)
POOL_BUF = max(POOL_WINDOWS) - 1
D_FF = 4 * D_MODEL
NORM_EPS = 1e-6
N_IN = 3 * D_ATTN + 2 * D_GMLP + D_POOL + 3 * D_MODEL
OFF_Q, OFF_K, OFF_V = 0, D_ATTN, 2 * D_ATTN
OFF_U = 3 * D_ATTN
OFF_VG = OFF_U + D_GMLP
OFF_XC = OFF_VG + D_GMLP
OFF_GATES = OFF_XC + D_POOL

LANES = 128
SUBLANES = 8
HEADS_PER_TILE = LANES // HEAD_DIM
ATTN_SCALE = HEAD_DIM ** -0.5
LOG2E = 1.4426950408889634
MASK_NEG = -(2.0 ** 100)
VMEM_LIMIT = 56 * 1024 * 1024
POOL_PREV_ROWS = 2 * SUBLANES
SAMPLES_PER_STEP = 2

NT_DIMS = (((1,), (1,)), ((), ()))


def _params(*sem):
    return pltpu.CompilerParams(dimension_semantics=sem, vmem_limit_bytes=VMEM_LIMIT)


def _resident(shape):
    nd = len(shape)
    return pl.BlockSpec(shape, lambda *_: (0,) * nd, pipeline_mode=pl.Buffered(1))


def _layer_resident(stacked_shape, layer):
    nd = len(stacked_shape) - 1
    return pl.BlockSpec((None,) + tuple(stacked_shape[1:]), lambda *_: (layer,) + (0,) * nd,
                        pipeline_mode=pl.Buffered(1))


def _split_bf16(x):
    hi = x.astype(BF16)
    return hi, (x - hi.astype(F32)).astype(BF16)


def _rms(x, g):
    return x * lax.rsqrt(jnp.mean(x * x, axis=-1, keepdims=True) + NORM_EPS) * g


def _proj_kernel(h_ref, g_ref, w_ref, wkv_t_ref, gn_ref, *refs, kv_transposed):
    q_ref, k_ref, v_ref, ug_ref, vn_ref, xc_ref, gates_ref = refs[-7:]
    xb = _rms(h_ref[...], g_ref[...]).astype(BF16)

    def seg(lo, width):
        return jnp.dot(xb, w_ref[:, lo:lo + width], preferred_element_type=F32)

    q_ref[...] = seg(OFF_Q, D_ATTN)
    if kv_transposed:
        k_ref[0, 0] = lax.dot_general(wkv_t_ref[0:D_ATTN, :], xb, NT_DIMS,
                                      preferred_element_type=F32)
        v_ref[0, 0] = lax.dot_general(wkv_t_ref[D_ATTN:2 * D_ATTN, :], xb, NT_DIMS,
                                      preferred_element_type=F32)
    else:
        k_ref[...] = seg(OFF_K, D_ATTN)
        v_ref[...] = seg(OFF_V, D_ATTN)
    ug_ref[...] = jax.nn.gelu(seg(OFF_U, D_GMLP))
    gv = jax.nn.gelu(seg(OFF_VG, D_GMLP))
    gc = gv - jnp.mean(gv, axis=-1, keepdims=True)
    vn_ref[...] = gc * lax.rsqrt(jnp.mean(gc * gc, axis=-1, keepdims=True) + NORM_EPS) * gn_ref[...]
    xc_ref[...] = seg(OFF_XC, D_POOL)
    for j in range(3):
        z = seg(OFF_GATES + j * D_MODEL, D_MODEL)
        gates_ref[:, j * D_MODEL:(j + 1) * D_MODEL] = jax.nn.sigmoid(z).astype(BF16)


def _proj(h, g, w_in, wkv_t, gn, layer, nseq, tm, kv_stacks=None):
    n = h.shape[0]
    t = n // nseq
    nt = t // tm
    row = lambda w: pl.BlockSpec((tm, w), lambda bi, ti: (bi * nt + ti, 0))
    flat = lambda w, dt=F32: jax.ShapeDtypeStruct((n, w), dt)
    in_specs = [row(D_MODEL)] + [_layer_resident(w.shape, layer) for w in (g, w_in, wkv_t, gn)]
    operands = [h, g, w_in, wkv_t, gn]
    if kv_stacks is not None:
        kv_spec = pl.BlockSpec((1, 1, D_ATTN, tm), lambda bi, ti: (bi, layer, 0, ti))
        kv_shape = jax.ShapeDtypeStruct(kv_stacks[0].shape, F32)
        aliases = {len(operands): 1, len(operands) + 1: 2}
        in_specs += [pl.BlockSpec(memory_space=pl.ANY)] * 2
        operands += list(kv_stacks)
    else:
        kv_spec, kv_shape, aliases = row(D_ATTN), flat(D_ATTN), {}
    return pl.pallas_call(
        functools.partial(_proj_kernel, kv_transposed=kv_stacks is not None),
        grid=(nseq, nt),
        in_specs=in_specs,
        out_specs=[row(D_ATTN), kv_spec, kv_spec, row(D_GMLP), row(D_GMLP), row(D_POOL),
                   row(3 * D_MODEL)],
        out_shape=[flat(D_ATTN), kv_shape, kv_shape, flat(D_GMLP), flat(D_GMLP), flat(D_POOL),
                   flat(3 * D_MODEL, BF16)],
        input_output_aliases=aliases,
        compiler_params=_params("parallel", "parallel"),
    )(*operands)


def _transpose_cast_kernel(w_ref, o_ref):
    o_ref[0] = w_ref[0].T.astype(BF16)


def _kv_weights_transposed(w_in):
    depth = w_in.shape[0]
    assert OFF_K % D_ATTN == 0 and OFF_V == OFF_K + D_ATTN
    return pl.pallas_call(
        _transpose_cast_kernel,
        grid=(depth, 2),
        in_specs=[pl.BlockSpec((1, D_MODEL, D_ATTN), lambda l, j: (l, 0, OFF_K // D_ATTN + j))],
        out_specs=pl.BlockSpec((1, D_ATTN, D_MODEL), lambda l, j: (l, j, 0)),
        out_shape=jax.ShapeDtypeStruct((depth, 2 * D_ATTN, D_MODEL), BF16),
        compiler_params=_params("parallel", "parallel"),
    )(w_in)


def _attn_prompt_kernel(q_ref, kt_ref, vt_ref, o_ref, kaug_ref, vb_ref, kbd_ref, qaug_ref, s_ref,
                        mx_ref, l_ref, acc_ref, *, k_top):
    i = pl.program_id(1)
    nb, n_pairs = kaug_ref.shape[0], kaug_ref.shape[1]
    nbp = SUBLANES
    fold = lambda x: (x[:, 0:LANES], x[:, LANES:2 * LANES])

    @pl.when(i == 0)
    def _():
        lane = lax.broadcasted_iota(jnp.int32, (LANES, LANES), 1)
        oh_row = lax.broadcasted_iota(jnp.int32, (LANES, MOBA_BLOCK), 0)
        half = lax.broadcasted_iota(jnp.int32, (nbp, LANES), 1) // HEAD_DIM
        kbd_ref[...] = jnp.zeros(kbd_ref.shape, BF16)
        for p in range(n_pairs):
            chans = slice(p * LANES, (p + 1) * LANES)
            km_t = jnp.zeros((LANES, LANES), F32)
            for n in range(nb):
                cols = slice(n * MOBA_BLOCK, (n + 1) * MOBA_BLOCK)
                ktn = kt_ref[0, chans, cols]
                kaug_ref[n, p, 0:LANES, :] = ktn.astype(BF16)
                onehot = (oh_row % nbp == n) & (oh_row // (HEADS_PER_TILE * nbp) == p)
                kaug_ref[n, p, LANES:2 * LANES, :] = jnp.where(onehot, 1.0, 0.0).astype(BF16)
                vb_ref[n, p] = vt_ref[0, chans, cols].astype(BF16)
                col = jnp.sum(ktn, axis=1, keepdims=True) * (1.0 / MOBA_BLOCK)
                km_t = jnp.where(lane == n, col, km_t)
            km = km_t.T
            km = jnp.concatenate([jnp.where(half == hh, km[0:nbp], 0.0)
                                  for hh in range(HEADS_PER_TILE)], axis=0)
            rows = slice(p * HEADS_PER_TILE * nbp, (p + 1) * HEADS_PER_TILE * nbp)
            km_hi, km_lo = _split_bf16(km)
            kbd_ref[0, rows, chans] = km_hi
            kbd_ref[1, rows, chans] = km_lo

    qf = q_ref[0]
    q_hi, q_lo = _split_bf16(qf)
    nt_dot = lambda a, b: lax.dot_general(a, b, NT_DIMS, preferred_element_type=F32)
    g = nt_dot(kbd_ref[0], q_hi) + (nt_dot(kbd_ref[0], q_lo) + nt_dot(kbd_ref[1], q_hi))
    gm = g.reshape(N_HEADS, nbp, MOBA_BLOCK)
    blk = lax.broadcasted_iota(jnp.int32, gm.shape, 1)
    past = blk < i
    gm = jnp.where(past, gm, -jnp.inf)
    rank = jnp.zeros(gm.shape, jnp.int32)
    for m in range(nb):
        gr = gm[:, m:m + 1, :]
        beats = (gr > gm) | ((gr == gm) & (m < blk))
        rank = rank + beats.astype(jnp.int32)
    sel = past & (rank < k_top) & (gm > -jnp.inf) & (gm < jnp.inf)
    sel_t = jnp.where(sel, 1.0, 0.0).reshape(N_HEADS * nbp, MOBA_BLOCK)
    sel_t = jnp.concatenate([sel_t, jnp.zeros((LANES - N_HEADS * nbp, MOBA_BLOCK), F32)], axis=0)
    rr = lax.broadcasted_iota(jnp.int32, (MOBA_BLOCK, MOBA_BLOCK), 0)
    cc = lax.broadcasted_iota(jnp.int32, (MOBA_BLOCK, MOBA_BLOCK), 1)
    eye = jnp.where(rr == cc, 1.0, 0.0).astype(BF16)
    sel_n = lax.dot_general(eye, sel_t.astype(BF16), NT_DIMS, preferred_element_type=F32)
    bias_all = jnp.where(sel_n > 0.5, 0.0, MASK_NEG)
    lane = lax.broadcasted_iota(jnp.int32, (MOBA_BLOCK, LANES), 1)
    for h in range(N_HEADS):
        p, hh = divmod(h, HEADS_PER_TILE)
        qh = jnp.where(lane // HEAD_DIM == hh, qf[:, p * LANES:(p + 1) * LANES], 0.0)
        qaug_ref[h, :, 0:LANES] = (qh * (ATTN_SCALE * LOG2E)).astype(BF16)
        qaug_ref[h, :, LANES:2 * LANES] = jnp.where(lane // nbp == h, bias_all, 0.0).astype(BF16)

    mx_ref[...] = jnp.full(mx_ref.shape, MASK_NEG, F32)

    def over_past_blocks(block_fn):
        def pair(j, carry):
            block_fn(2 * j)
            block_fn(2 * j + 1)
            return carry

        lax.fori_loop(0, i // 2, pair, 0)

        @pl.when(i % 2 == 1)
        def _():
            block_fn(i - 1)

    def scores(n):
        for h in range(N_HEADS):
            s = jnp.dot(qaug_ref[h], kaug_ref[n, h // HEADS_PER_TILE],
                        preferred_element_type=F32)
            s_ref[h, n] = s
            lo, hi = fold(s)
            mx_ref[h] = jnp.maximum(mx_ref[h], jnp.maximum(lo, hi))

    over_past_blocks(scores)

    causal = cc <= rr
    for h in range(N_HEADS):
        p = h // HEADS_PER_TILE
        s_own = jnp.dot(qaug_ref[h, :, 0:LANES], kaug_ref[i, p, 0:LANES, :],
                        preferred_element_type=F32)
        s_own = jnp.where(causal, s_own, MASK_NEG)
        lo, hi = fold(s_own)
        mx = jnp.max(jnp.maximum(mx_ref[h], jnp.maximum(lo, hi)), axis=-1, keepdims=True)
        mx_ref[h] = jnp.broadcast_to(mx, (MOBA_BLOCK, LANES))
        p_own = jnp.exp2(s_own - mx)
        lo, hi = fold(p_own)
        l_ref[h] = lo + hi
        acc_ref[h] = lax.dot_general(p_own.astype(BF16), vb_ref[i, p], NT_DIMS,
                                     preferred_element_type=F32)

    def values(n):
        for h in range(N_HEADS):
            mx = mx_ref[h]
            lo, hi = fold(s_ref[h, n])
            p_lo = jnp.exp2(lo - mx)
            p_hi = jnp.exp2(hi - mx)
            l_ref[h] += p_lo + p_hi
            pb = jnp.concatenate([p_lo, p_hi], axis=1).astype(BF16)
            acc_ref[h] += lax.dot_general(pb, vb_ref[n, h // HEADS_PER_TILE], NT_DIMS,
                                          preferred_element_type=F32)

    over_past_blocks(values)

    for p in range(n_pairs):
        o = [acc_ref[p * HEADS_PER_TILE + hh]
             / jnp.sum(l_ref[p * HEADS_PER_TILE + hh], axis=-1, keepdims=True)
             for hh in range(HEADS_PER_TILE)]
        o_ref[0, :, p * LANES:(p + 1) * LANES] = jnp.where(lane < HEAD_DIM, o[0], o[1])


def _attn_prompt(q, kt, vt, layer):
    b, t, _ = q.shape
    assert t % MOBA_BLOCK == 0
    nb = t // MOBA_BLOCK
    n_pairs = D_ATTN // LANES
    assert nb <= SUBLANES and N_HEADS * SUBLANES <= LANES
    qspec = pl.BlockSpec((1, MOBA_BLOCK, D_ATTN), lambda bi, i: (bi, i, 0))
    kspec = pl.BlockSpec((1, None, D_ATTN, t), lambda bi, i: (bi, layer, 0, 0))
    head_acc = pltpu.VMEM((N_HEADS, MOBA_BLOCK, LANES), F32)
    return pl.pallas_call(
        functools.partial(_attn_prompt_kernel, k_top=min(MOBA_TOP_K, nb)),
        grid=(b, nb),
        in_specs=[qspec, kspec, kspec],
        out_specs=qspec,
        out_shape=jax.ShapeDtypeStruct((b, t, D_ATTN), F32),
        scratch_shapes=[pltpu.VMEM((nb, n_pairs, 2 * LANES, MOBA_BLOCK), BF16),
                        pltpu.VMEM((nb, n_pairs, LANES, MOBA_BLOCK), BF16),
                        pltpu.VMEM((2, N_HEADS * SUBLANES, D_ATTN), BF16),
                        pltpu.VMEM((N_HEADS, MOBA_BLOCK, 2 * LANES), BF16),
                        pltpu.VMEM((N_HEADS, nb, MOBA_BLOCK, MOBA_BLOCK), F32),
                        head_acc, head_acc, head_acc],
        compiler_params=_params("parallel", "arbitrary"),
    )(q, kt, vt)


def _attn_sample_kernel(pt_ref, q_ref, kn_ref, vn_ref, *refs, n_pages, k_top):
    del pt_ref
    o_ref = refs[-1]
    for s in range(q_ref.shape[0]):
        kp = refs[2 * s * n_pages:(2 * s + 1) * n_pages]
        vp = refs[(2 * s + 1) * n_pages:(2 * s + 2) * n_pages]
        o_ref[s] = _attend_sample(q_ref[s], kn_ref[s], vn_ref[s], kp, vp, k_top)


def _attend_sample(q, kn, vn, kp, vp, k_top):
    n_pages = len(kp)
    ts = q.shape[0]
    ppb = MOBA_BLOCK // kp[0].shape[3]
    nb = n_pages // ppb
    rows = N_HEADS * ts
    block = lambda pages, n: jnp.concatenate([pages[n * ppb + j][0, 0] for j in range(ppb)], axis=1)

    lane_head = lax.broadcasted_iota(jnp.int32, (ts, D_ATTN), 1) // HEAD_DIM
    q_bd = jnp.concatenate([jnp.where(lane_head == h, q, 0.0) for h in range(N_HEADS)], axis=0)
    qs = (q_bd * ATTN_SCALE).astype(BF16)

    lane = lax.broadcasted_iota(jnp.int32, (D_ATTN, LANES), 1)
    km_t = jnp.zeros((D_ATTN, LANES), F32)
    s_list = []
    for n in range(nb):
        kt = block(kp, n)
        col = jnp.sum(kt, axis=1, keepdims=True) * (1.0 / MOBA_BLOCK)
        km_t = jnp.where(lane == n, col, km_t)
        s_list.append(jnp.dot(qs, kt.astype(BF16), preferred_element_type=F32))

    gate = jnp.dot(q_bd, km_t, precision=lax.Precision.HIGHEST, preferred_element_type=F32)
    blk = lax.broadcasted_iota(jnp.int32, (rows, LANES), 1)
    gm = jnp.where(blk < nb, gate, -jnp.inf)
    rank = jnp.zeros((rows, LANES), jnp.int32)
    for m in range(nb):
        gc = gm[:, m:m + 1]
        beats = (gc > gm) | ((gc == gm) & (m < blk))
        rank = rank + beats.astype(jnp.int32)
    sel = (rank < k_top) & (gm > -jnp.inf) & (gm < jnp.inf)
    bias = jnp.where(sel, 0.0, MASK_NEG)

    pad = jnp.zeros((LANES - ts, D_ATTN), F32)
    k_new = jnp.concatenate([kn, pad], axis=0).astype(BF16)
    v_new = jnp.concatenate([vn, pad], axis=0).astype(BF16)
    s_new = lax.dot_general(qs, k_new, NT_DIMS, preferred_element_type=F32)
    qi = lax.broadcasted_iota(jnp.int32, (rows, LANES), 0) % ts
    s_new = jnp.where(blk <= qi, s_new, MASK_NEG)

    mx = jnp.max(s_new, axis=-1, keepdims=True)
    for n in range(nb):
        s_list[n] = s_list[n] + bias[:, n:n + 1]
        mx = jnp.maximum(mx, jnp.max(s_list[n], axis=-1, keepdims=True))
    p_new = jnp.exp(s_new - mx)
    den = jnp.sum(p_new, axis=-1, keepdims=True)
    acc = jnp.dot(p_new.astype(BF16), v_new, preferred_element_type=F32)
    for n in range(nb):
        p_n = jnp.exp(s_list[n] - mx)
        den = den + jnp.sum(p_n, axis=-1, keepdims=True)
        vt = block(vp, n).astype(BF16)
        acc = acc + lax.dot_general(p_n.astype(BF16), vt, NT_DIMS, preferred_element_type=F32)
    acc = acc / den
    out = jnp.zeros((ts, D_ATTN), F32)
    for h in range(N_HEADS):
        out = out + jnp.where(lane_head == h, acc[h * ts:(h + 1) * ts, :], 0.0)
    return out


def _attn_sample(q, k_new, v_new, cache_kt, cache_vt, page_table, layer):
    b, ts, _ = q.shape
    n_pages = page_table.shape[1]
    page = cache_kt.shape[3]
    past_len = n_pages * page
    assert ts == SUBLANES and MOBA_BLOCK % page == 0 and past_len % MOBA_BLOCK == 0
    nb = past_len // MOBA_BLOCK
    assert nb <= LANES
    spb = SAMPLES_PER_STEP if b % SAMPLES_PER_STEP == 0 else 1
    tok = pl.BlockSpec((spb, ts, D_ATTN), lambda bi, pt: (bi, 0, 0))

    def page_spec(s, j):
        return pl.BlockSpec((1, 1, D_ATTN, page), lambda bi, pt: (pt[bi * spb + s, j], layer, 0, 0))

    pages, caches = [], []
    for s in range(spb):
        pages += [page_spec(s, j) for j in range(n_pages)] * 2
        caches += [cache_kt] * n_pages + [cache_vt] * n_pages
    return pl.pallas_call(
        functools.partial(_attn_sample_kernel, n_pages=n_pages, k_top=min(MOBA_TOP_K, nb)),
        grid_spec=pltpu.PrefetchScalarGridSpec(
            num_scalar_prefetch=1,
            grid=(b // spb,),
            in_specs=[tok, tok, tok] + pages,
            out_specs=tok,
        ),
        out_shape=jax.ShapeDtypeStruct((b, ts, D_ATTN), F32),
        compiler_params=_params("parallel"),
    )(page_table, q, k_new, v_new, *caches)


def _lane_windows(shape):
    grp = lax.broadcasted_iota(jnp.int32, shape, len(shape) - 1) // (D_POOL // POOL_GROUPS)
    win = jnp.full(shape, POOL_WINDOWS[0], jnp.int32)
    for gi in range(1, POOL_GROUPS):
        win = jnp.where(grp == gi, POOL_WINDOWS[gi], win)
    return win


def _prompt_branches(ti, ug_ref, vn_ref, xc_ref, prev_ref, ws_ref, bs_ref, wp_ref, ps_ref):
    tm = ug_ref.shape[0]
    dg = D_GMLP // GMLP_GROUPS

    wr = lax.broadcasted_iota(jnp.int32, (GMLP_CHUNK, GMLP_GROUPS * GMLP_CHUNK), 0)
    wc = lax.broadcasted_iota(jnp.int32, (GMLP_CHUNK, GMLP_GROUPS * GMLP_CHUNK), 1) % GMLP_CHUNK
    w_tril = jnp.where(wc <= wr, ws_ref[...], 0.0).astype(BF16)
    lane_grp = lax.broadcasted_iota(jnp.int32, (GMLP_CHUNK, D_GMLP), 1) // dg
    b_chunks = []
    for c in range(tm // GMLP_CHUNK):
        rows = slice(c * GMLP_CHUNK, (c + 1) * GMLP_CHUNK)
        vc = vn_ref[rows, :]
        v_stack = jnp.concatenate([jnp.where(lane_grp == gi, vc, 0.0) for gi in range(GMLP_GROUPS)],
                                  axis=0).astype(BF16)
        mixed = jnp.dot(w_tril, v_stack, preferred_element_type=F32) + bs_ref[...]
        b_chunks.append(ug_ref[rows, :] * mixed)
    b_out = jnp.concatenate(b_chunks, axis=0)

    x = xc_ref[...]
    prev = jnp.where(ti > 0, prev_ref[...], 0.0)
    xe = jnp.concatenate([prev, x], axis=0)
    s2 = xe + pltpu.roll(xe, 1, axis=0)
    s4 = s2 + pltpu.roll(s2, 2, axis=0)
    s8 = s4 + pltpu.roll(s4, 4, axis=0)
    s16 = s8 + pltpu.roll(s8, 8, axis=0)
    win = _lane_windows((tm, D_POOL))
    pre = POOL_PREV_ROWS
    wsum = jnp.where(win == 2, s2[pre:], jnp.where(win == 4, s4[pre:],
                     jnp.where(win == 8, s8[pre:], s16[pre:])))
    pos = ti * tm + lax.broadcasted_iota(jnp.int32, (tm, D_POOL), 0)
    count = jnp.minimum(win, pos + 1).astype(F32)
    pooled = wsum / count - x
    y = jnp.dot(pooled.astype(BF16), wp_ref[...], preferred_element_type=F32)
    return b_out, y * ps_ref[...]


def _branch_sample_kernel(ug_ref, vn_ref, xc_ref, pre_ref, wc_ref, bs_ref, wp_ref, ps_ref,
                          b_ref, c_ref, *, pos0):
    nb, ts, _ = ug_ref.shape
    vn = vn_ref[...]
    t_idx = lax.broadcasted_iota(jnp.int32, (ts, D_GMLP), 0)
    mixed = jnp.zeros((nb, ts, D_GMLP), F32) + bs_ref[...][None]
    for s in range(ts):
        coef = jnp.where(s <= t_idx, wc_ref[s], 0.0)
        mixed = mixed + coef[None] * vn[:, s:s + 1, :]
    b_ref[...] = ug_ref[...] * mixed

    x = xc_ref[...]
    pre = pre_ref[...]
    win = _lane_windows((ts, D_POOL))
    end = POOL_BUF + t_idx
    wsum = jnp.zeros((nb, ts, D_POOL), F32)
    for r in range(POOL_BUF + ts):
        row = pre[:, r:r + 1, :] if r < POOL_BUF else x[:, r - POOL_BUF:r - POOL_BUF + 1, :]
        inside = (r <= end) & (r > end - win)
        wsum = wsum + jnp.where(inside, 1.0, 0.0)[None] * row
    count = jnp.minimum(win, pos0 + t_idx + 1).astype(F32)
    pooled = wsum / count[None] - x
    y = jnp.dot(pooled.reshape(nb * ts, D_POOL).astype(BF16), wp_ref[...], preferred_element_type=F32)
    c_ref[...] = (y * ps_ref[...]).reshape(nb, ts, D_POOL)


def _branch_sample(ug, vn, xc, prefix, wc, bs_t, wp_bd, ps, layer, pos0):
    b, ts, _ = ug.shape
    assert ts == SUBLANES
    full = lambda shape: pl.BlockSpec(shape, lambda i: (0,) * len(shape))
    return pl.pallas_call(
        functools.partial(_branch_sample_kernel, pos0=pos0),
        grid=(1,),
        in_specs=[full(ug.shape), full(vn.shape), full(xc.shape), full(prefix.shape)]
        + [_layer_resident(w.shape, layer) for w in (wc, bs_t, wp_bd, ps)],
        out_specs=[full(ug.shape), full(xc.shape)],
        out_shape=[jax.ShapeDtypeStruct(ug.shape, F32), jax.ShapeDtypeStruct(xc.shape, F32)],
        compiler_params=_params("arbitrary"),
    )(ug, vn, xc, prefix, wc, bs_t, wp_bd, ps)


def _merge_mlp_kernel(h_ref, a_ref, *refs, final, ff_chunk, n_branch_refs):
    branch_refs = refs[:n_branch_refs]
    gates_ref, wa_ref, wb_ref, wc_ref, wo_ref, g_ref, wu_ref, wd_ref, gf_ref, o_ref = refs[n_branch_refs:]
    if n_branch_refs == 2:
        b_out, c_out = branch_refs[0][...], branch_refs[1][...]
    else:
        b_out, c_out = _prompt_branches(pl.program_id(1), *branch_refs)

    def branch(x, w_ref, j):
        y = jnp.dot(x.astype(BF16), w_ref[...], preferred_element_type=F32)
        return gates_ref[:, j * D_MODEL:(j + 1) * D_MODEL].astype(F32) * y

    merged = branch(a_ref[...], wa_ref, 0) + branch(b_out, wb_ref, 1) + branch(c_out, wc_ref, 2)
    h1 = h_ref[...] + jnp.dot(merged.astype(BF16), wo_ref[...], preferred_element_type=F32)
    xb = _rms(h1, g_ref[...]).astype(BF16)
    acc = h1
    for c in range(D_FF // ff_chunk):
        cols = slice(c * ff_chunk, (c + 1) * ff_chunk)
        up = jnp.dot(xb, wu_ref[:, cols], preferred_element_type=F32)
        act = jnp.square(jnp.maximum(up, 0.0)).astype(BF16)
        acc = acc + jnp.dot(act, wd_ref[cols, :], preferred_element_type=F32)
    o_ref[...] = _rms(acc, gf_ref[...]) if final else acc


def _merge_mlp(h, a, branches, gates, wa, wb, wc, wo, g, wu, wd, gf, layer, final, nseq, tm):
    n = h.shape[0]
    nt = n // nseq // tm
    row = lambda w: pl.BlockSpec((tm, w), lambda bi, ti: (bi * nt + ti, 0))
    stack = lambda w: _layer_resident(w.shape, layer)
    if len(branches) == 2:
        branch_specs, branch_ops = [row(D_GMLP), row(D_POOL)], list(branches)
    else:
        ug, vn, xc, ws_cat, bs_t, wp_bd, ps = branches
        assert tm % GMLP_CHUNK == 0 and tm % POOL_PREV_ROWS == 0
        per = tm // POOL_PREV_ROWS
        prev = pl.BlockSpec((POOL_PREV_ROWS, D_POOL),
                            lambda bi, ti: (jnp.maximum((bi * nt + ti) * per - 1, 0), 0))
        branch_specs = [row(D_GMLP), row(D_GMLP), row(D_POOL), prev] + [
            _layer_resident(w.shape, layer) for w in (ws_cat, bs_t, wp_bd, ps)]
        branch_ops = [ug, vn, xc, xc, ws_cat, bs_t, wp_bd, ps]
    return pl.pallas_call(
        functools.partial(_merge_mlp_kernel, final=final, ff_chunk=1024,
                          n_branch_refs=len(branch_ops)),
        grid=(nseq, nt),
        in_specs=[row(D_MODEL), row(D_ATTN)] + branch_specs + [
            row(3 * D_MODEL), stack(wa), stack(wb), stack(wc), stack(wo), stack(g),
            stack(wu), stack(wd), _resident((1, D_MODEL))],
        out_specs=row(D_MODEL),
        out_shape=jax.ShapeDtypeStruct((n, D_MODEL), F32),
        compiler_params=_params("parallel", "parallel"),
    )(h, a, *branch_ops, gates, wa, wb, wc, wo, g, wu, wd, gf)


def _block_diag(w):
    n, g, d, e = w.shape
    eye = jnp.eye(g, dtype=w.dtype)
    return (eye[None, :, None, :, None] * w[:, :, :, None, :]).reshape(n, g * d, g * e)


def _pages_channel_major(cache):
    n_phys, depth, page, _, _ = cache.shape
    return jnp.transpose(cache, (0, 1, 3, 4, 2)).reshape(n_phys, depth, D_ATTN, page)


def _heads_last(kt):
    b, depth, _, t = kt.shape
    return jnp.transpose(kt.reshape(b, depth, N_HEADS, HEAD_DIM, t), (0, 1, 4, 2, 3))


def kernel(x_prompt, x_sample, cache_k, cache_v, state_pool, page_table, norm1_g, w_in, gmlp_norm_g,
           w_spatial, b_spatial, w_pool, pool_scale, w_branch_a, w_branch_b, w_branch_c, w_out,
           norm2_g, w_up, w_down, final_norm_g):
    bp, tp, _ = x_prompt.shape
    bs, ts, _ = x_sample.shape
    depth = w_in.shape[0]
    past_len = page_table.shape[1] * cache_k.shape[2]
    dg = D_GMLP // GMLP_GROUPS
    tm = min(512, tp)
    tms = min(512, bs * ts)

    ckt = _pages_channel_major(cache_k)
    cvt = _pages_channel_major(cache_v)
    w_in_b = w_in.astype(BF16)
    wkv_t = _kv_weights_transposed(w_in)
    wa_b, wb_b, wc_b = w_branch_a.astype(BF16), w_branch_b.astype(BF16), w_branch_c.astype(BF16)
    wo_b, wu_b, wd_b = w_out.astype(BF16), w_up.astype(BF16), w_down.astype(BF16)
    gf = final_norm_g.reshape(1, D_MODEL)
    g1 = norm1_g.reshape(depth, 1, D_MODEL)
    g2 = norm2_g.reshape(depth, 1, D_MODEL)
    gn = gmlp_norm_g.reshape(depth, 1, D_GMLP)
    ps = pool_scale.reshape(depth, 1, D_POOL)
    wp_bd = _block_diag(w_pool).astype(BF16)
    ws_cat = jnp.transpose(w_spatial, (0, 2, 1, 3)).reshape(depth, GMLP_CHUNK, GMLP_GROUPS * GMLP_CHUNK)
    bs_full = jnp.repeat(jnp.transpose(b_spatial, (0, 2, 1)), dg, axis=2)
    wc_s = jnp.repeat(jnp.transpose(w_spatial[:, :, :ts, :ts], (0, 3, 2, 1)), dg, axis=3)
    bs_s = bs_full[:, :ts]

    hp = x_prompt.reshape(bp * tp, D_MODEL)
    hs = x_sample.reshape(bs * ts, D_MODEL)
    outs = {n: [] for n in ("pp", "ks", "vs", "ps", "gv")}
    kv_stacks = (jnp.zeros((bp, depth, D_ATTN, tp), F32), jnp.zeros((bp, depth, D_ATTN, tp), F32))
    for l in range(depth):
        q, kst, vst, ug, vn, xc, gates = _proj(hp, g1, w_in_b, wkv_t, gn, l, bp, tm, kv_stacks)
        kv_stacks = (kst, vst)
        r3 = lambda a: a.reshape(bp, tp, a.shape[-1])
        a_out = _attn_prompt(r3(q), kst, vst, l)
        hp = _merge_mlp(hp, a_out.reshape(bp * tp, D_ATTN), (ug, vn, xc, ws_cat, bs_full, wp_bd, ps),
                        gates, wa_b, wb_b, wc_b, wo_b, g2, wu_b, wd_b, gf, l, l == depth - 1, bp, tm)
        outs["pp"].append(r3(xc)[:, tp - POOL_BUF:])

        q, k, v, ug, vn, xc, gates = _proj(hs, g1, w_in_b, wkv_t, gn, l, 1, tms)
        s3 = lambda a: a.reshape(bs, ts, a.shape[-1])
        a_out = _attn_sample(s3(q), s3(k), s3(v), ckt, cvt, page_table, l)
        prefix = state_pool[:, l]
        b_out, c_out = _branch_sample(s3(ug), s3(vn), s3(xc), prefix, wc_s, bs_s, wp_bd, ps, l, past_len)
        hs = _merge_mlp(hs, a_out.reshape(bs * ts, D_ATTN),
                        (b_out.reshape(bs * ts, D_GMLP), c_out.reshape(bs * ts, D_POOL)),
                        gates, wa_b, wb_b, wc_b, wo_b, g2, wu_b, wd_b, gf, l, l == depth - 1, 1, tms)
        outs["ks"].append(k.reshape(bs, ts, N_HEADS, HEAD_DIM))
        outs["vs"].append(v.reshape(bs, ts, N_HEADS, HEAD_DIM))
        outs["ps"].append(jnp.concatenate([prefix, s3(xc)], axis=1)[:, -POOL_BUF:])
        outs["gv"].append(s3(vn))

    stack = lambda n: jnp.stack(outs[n], axis=1)
    return (hp.reshape(bp, tp, D_MODEL), hs.reshape(bs, ts, D_MODEL), _heads_last(kv_stacks[0]),
            _heads_last(kv_stacks[1]), stack("pp"), stack("ks"), stack("vs"), stack("ps"), stack("gv"))
```

```python
import functools

import jax
import jax.numpy as jnp
from jax import lax
from jax.experimental import pallas as pl
from jax.experimental.pallas import tpu as pltpu

F32 = jnp.float32
BF16 = jnp.bfloat16

D_MODEL = 1024
N_HEADS = 8
HEAD_DIM = 64
D_ATTN = N_HEADS * HEAD_DIM
MOBA_BLOCK = 256
MOBA_TOP_K = 3
GMLP_GROUPS = 4
D_GMLP = 256
GMLP_CHUNK = 128
POOL_GROUPS = 4
D_POOL = 256
POOL_WINDOWS = (2, 4, 8, 16)
POOL_BUF = max(POOL_WINDOWS) - 1
D_FF = 4 * D_MODEL
NORM_EPS = 1e-6
N_IN = 3 * D_ATTN + 2 * D_GMLP + D_POOL + 3 * D_MODEL
OFF_Q, OFF_K, OFF_V = 0, D_ATTN, 2 * D_ATTN
OFF_U = 3 * D_ATTN
OFF_VG = OFF_U + D_GMLP
OFF_XC = OFF_VG + D_GMLP
OFF_GATES = OFF_XC + D_POOL

LANES = 128
SUBLANES = 8
HEADS_PER_TILE = LANES // HEAD_DIM
ATTN_SCALE = HEAD_DIM ** -0.5
LOG2E = 1.4426950408889634
MASK_NEG = -(2.0 ** 100)
VMEM_LIMIT = 56 * 1024 * 1024
POOL_PREV_ROWS = 2 * SUBLANES
SAMPLES_PER_STEP = 2

NT_DIMS = (((1,), (1,)), ((), ()))


def _params(*sem):
    return pltpu.CompilerParams(dimension_semantics=sem, vmem_limit_bytes=VMEM_LIMIT)


def _resident(shape):
    nd = len(shape)
    return pl.BlockSpec(shape, lambda *_: (0,) * nd, pipeline_mode=pl.Buffered(1))


def _layer_resident(stacked_shape, layer):
    nd = len(stacked_shape) - 1
    return pl.BlockSpec((None,) + tuple(stacked_shape[1:]), lambda *_: (layer,) + (0,) * nd,
                        pipeline_mode=pl.Buffered(1))


def _split_bf16(x):
    hi = x.astype(BF16)
    return hi, (x - hi.astype(F32)).astype(BF16)


def _rms(x, g):
    return x * lax.rsqrt(jnp.mean(x * x, axis=-1, keepdims=True) + NORM_EPS) * g


def _proj_kernel(h_ref, g_ref, w_ref, wkv_t_ref, gn_ref, *refs, kv_transposed):
    q_ref, k_ref, v_ref, ug_ref, vn_ref, xc_ref, gates_ref = refs[-7:]
    xb = _rms(h_ref[...], g_ref[...]).astype(BF16)

    def seg(lo, width):
        return jnp.dot(xb, w_ref[:, lo:lo + width], preferred_element_type=F32)

    q_ref[...] = seg(OFF_Q, D_ATTN)
    if kv_transposed:
        k_ref[0, 0] = lax.dot_general(wkv_t_ref[0:D_ATTN, :], xb, NT_DIMS,
                                      preferred_element_type=F32)
        v_ref[0, 0] = lax.dot_general(wkv_t_ref[D_ATTN:2 * D_ATTN, :], xb, NT_DIMS,
                                      preferred_element_type=F32)
    else:
        k_ref[...] = seg(OFF_K, D_ATTN)
        v_ref[...] = seg(OFF_V, D_ATTN)
    ug_ref[...] = jax.nn.gelu(seg(OFF_U, D_GMLP))
    gv = jax.nn.gelu(seg(OFF_VG, D_GMLP))
    gc = gv - jnp.mean(gv, axis=-1, keepdims=True)
    vn_ref[...] = gc * lax.rsqrt(jnp.mean(gc * gc, axis=-1, keepdims=True) + NORM_EPS) * gn_ref[...]
    xc_ref[...] = seg(OFF_XC, D_POOL)
    for j in range(3):
        z = seg(OFF_GATES + j * D_MODEL, D_MODEL)
        gates_ref[:, j * D_MODEL:(j + 1) * D_MODEL] = jax.nn.sigmoid(z).astype(BF16)


def _proj(h, g, w_in, wkv_t, gn, layer, nseq, tm, kv_stacks=None):
    n = h.shape[0]
    t = n // nseq
    nt = t // tm
    row = lambda w: pl.BlockSpec((tm, w), lambda bi, ti: (bi * nt + ti, 0))
    flat = lambda w, dt=F32: jax.ShapeDtypeStruct((n, w), dt)
    in_specs = [row(D_MODEL)] + [_layer_resident(w.shape, layer) for w in (g, w_in, wkv_t, gn)]
    operands = [h, g, w_in, wkv_t, gn]
    if kv_stacks is not None:
        kv_spec = pl.BlockSpec((1, 1, D_ATTN, tm), lambda bi, ti: (bi, layer, 0, ti))
        kv_shape = jax.ShapeDtypeStruct(kv_stacks[0].shape, F32)
        aliases = {len(operands): 1, len(operands) + 1: 2}
        in_specs += [pl.BlockSpec(memory_space=pl.ANY)] * 2
        operands += list(kv_stacks)
    else:
        kv_spec, kv_shape, aliases = row(D_ATTN), flat(D_ATTN), {}
    return pl.pallas_call(
        functools.partial(_proj_kernel, kv_transposed=kv_stacks is not None),
        grid=(nseq, nt),
        in_specs=in_specs,
        out_specs=[row(D_ATTN), kv_spec, kv_spec, row(D_GMLP), row(D_GMLP), row(D_POOL),
                   row(3 * D_MODEL)],
        out_shape=[flat(D_ATTN), kv_shape, kv_shape, flat(D_GMLP), flat(D_GMLP), flat(D_POOL),
                   flat(3 * D_MODEL, BF16)],
        input_output_aliases=aliases,
        compiler_params=_params("parallel", "parallel"),
    )(*operands)


def _transpose_cast_kernel(w_ref, o_ref):
    o_ref[0] = w_ref[0].T.astype(BF16)


def _kv_weights_transposed(w_in):
    depth = w_in.shape[0]
    assert OFF_K % D_ATTN == 0 and OFF_V == OFF_K + D_ATTN
    return pl.pallas_call(
        _transpose_cast_kernel,
        grid=(depth, 2),
        in_specs=[pl.BlockSpec((1, D_MODEL, D_ATTN), lambda l, j: (l, 0, OFF_K // D_ATTN + j))],
        out_specs=pl.BlockSpec((1, D_ATTN, D_MODEL), lambda l, j: (l, j, 0)),
        out_shape=jax.ShapeDtypeStruct((depth, 2 * D_ATTN, D_MODEL), BF16),
        compiler_params=_params("parallel", "parallel"),
    )(w_in)


def _attn_prompt_kernel(q_ref, kt_ref, vt_ref, o_ref, kaug_ref, vb_ref, kbd_ref, qaug_ref, s_ref,
                        mx_ref, l_ref, acc_ref, *, k_top):
    i = pl.program_id(1)
    nb, n_pairs = kaug_ref.shape[0], kaug_ref.shape[1]
    nbp = SUBLANES
    fold = lambda x: (x[:, 0:LANES], x[:, LANES:2 * LANES])

    @pl.when(i == 0)
    def _():
        lane = lax.broadcasted_iota(jnp.int32, (LANES, LANES), 1)
        oh_row = lax.broadcasted_iota(jnp.int32, (LANES, MOBA_BLOCK), 0)
        half = lax.broadcasted_iota(jnp.int32, (nbp, LANES), 1) // HEAD_DIM
        kbd_ref[...] = jnp.zeros(kbd_ref.shape, BF16)
        for p in range(n_pairs):
            chans = slice(p * LANES, (p + 1) * LANES)
            km_t = jnp.zeros((LANES, LANES), F32)
            for n in range(nb):
                cols = slice(n * MOBA_BLOCK, (n + 1) * MOBA_BLOCK)
                ktn = kt_ref[0, chans, cols]
                kaug_ref[n, p, 0:LANES, :] = ktn.astype(BF16)
                onehot = (oh_row % nbp == n) & (oh_row // (HEADS_PER_TILE * nbp) == p)
                kaug_ref[n, p, LANES:2 * LANES, :] = jnp.where(onehot, 1.0, 0.0).astype(BF16)
                vb_ref[n, p] = vt_ref[0, chans, cols].astype(BF16)
                col = jnp.sum(ktn, axis=1, keepdims=True) * (1.0 / MOBA_BLOCK)
                km_t = jnp.where(lane == n, col, km_t)
            km = km_t.T
            km = jnp.concatenate([jnp.where(half == hh, km[0:nbp], 0.0)
                                  for hh in range(HEADS_PER_TILE)], axis=0)
            rows = slice(p * HEADS_PER_TILE * nbp, (p + 1) * HEADS_PER_TILE * nbp)
            km_hi, km_lo = _split_bf16(km)
            kbd_ref[0, rows, chans] = km_hi
            kbd_ref[1, rows, chans] = km_lo

    qf = q_ref[0]
    q_hi, q_lo = _split_bf16(qf)
    nt_dot = lambda a, b: lax.dot_general(a, b, NT_DIMS, preferred_element_type=F32)
    g = nt_dot(kbd_ref[0], q_hi) + (nt_dot(kbd_ref[0], q_lo) + nt_dot(kbd_ref[1], q_hi))
    gm = g.reshape(N_HEADS, nbp, MOBA_BLOCK)
    blk = lax.broadcasted_iota(jnp.int32, gm.shape, 1)
    past = blk < i
    gm = jnp.where(past, gm, -jnp.inf)
    rank = jnp.zeros(gm.shape, jnp.int32)
    for m in range(nb):
        gr = gm[:, m:m + 1, :]
        beats = (gr > gm) | ((gr == gm) & (m < blk))
        rank = rank + beats.astype(jnp.int32)
    sel = past & (rank < k_top) & (gm > -jnp.inf) & (gm < jnp.inf)
    sel_t = jnp.where(sel, 1.0, 0.0).reshape(N_HEADS * nbp, MOBA_BLOCK)
    sel_t = jnp.concatenate([sel_t, jnp.zeros((LANES - N_HEADS * nbp, MOBA_BLOCK), F32)], axis=0)
    rr = lax.broadcasted_iota(jnp.int32, (MOBA_BLOCK, MOBA_BLOCK), 0)
    cc = lax.broadcasted_iota(jnp.int32, (MOBA_BLOCK, MOBA_BLOCK), 1)
    eye = jnp.where(rr == cc, 1.0, 0.0).astype(BF16)
    sel_n = lax.dot_general(eye, sel_t.astype(BF16), NT_DIMS, preferred_element_type=F32)
    bias_all = jnp.where(sel_n > 0.5, 0.0, MASK_NEG)
    lane = lax.broadcasted_iota(jnp.int32, (MOBA_BLOCK, LANES), 1)
    for h in range(N_HEADS):
        p, hh = divmod(h, HEADS_PER_TILE)
        qh = jnp.where(lane // HEAD_DIM == hh, qf[:, p * LANES:(p + 1) * LANES], 0.0)
        qaug_ref[h, :, 0:LANES] = (qh * (ATTN_SCALE * LOG2E)).astype(BF16)
        qaug_ref[h, :, LANES:2 * LANES] = jnp.where(lane // nbp == h, bias_all, 0.0).astype(BF16)

    mx_ref[...] = jnp.full(mx_ref.shape, MASK_NEG, F32)

    def over_past_blocks(block_fn):
        def quad(j, carry):
            for d in range(4):
                block_fn(4 * j + d)
            return carry

        lax.fori_loop(0, i // 4, quad, 0)
        rem = i % 4

        @pl.when(rem >= 2)
        def _():
            block_fn(i - rem)
            block_fn(i - rem + 1)

        @pl.when(rem % 2 == 1)
        def _():
            block_fn(i - 1)

    def scores(n):
        for h in range(N_HEADS):
            s = jnp.dot(qaug_ref[h], kaug_ref[n, h // HEADS_PER_TILE],
                        preferred_element_type=F32)
            s_ref[h, n] = s
            lo, hi = fold(s)
            mx_ref[h] = jnp.maximum(mx_ref[h], jnp.maximum(lo, hi))

    over_past_blocks(scores)

    causal = cc <= rr
    for h in range(N_HEADS):
        p = h // HEADS_PER_TILE
        s_own = jnp.dot(qaug_ref[h, :, 0:LANES], kaug_ref[i, p, 0:LANES, :],
                        preferred_element_type=F32)
        s_own = jnp.where(causal, s_own, MASK_NEG)
        lo, hi = fold(s_own)
        mx = jnp.max(jnp.maximum(mx_ref[h], jnp.maximum(lo, hi)), axis=-1, keepdims=True)
        mx_ref[h] = jnp.broadcast_to(mx, (MOBA_BLOCK, LANES))
        p_own = jnp.exp2(s_own - mx)
        lo, hi = fold(p_own)
        l_ref[h] = lo + hi
        acc_ref[h] = lax.dot_general(p_own.astype(BF16), vb_ref[i, p], NT_DIMS,
                                     preferred_element_type=F32)

    def values(n):
        for h in range(N_HEADS):
            mx = mx_ref[h]
            lo, hi = fold(s_ref[h, n])
            p_lo = jnp.exp2(lo - mx)
            p_hi = jnp.exp2(hi - mx)
            l_ref[h] += p_lo + p_hi
            pb = jnp.concatenate([p_lo, p_hi], axis=1).astype(BF16)
            acc_ref[h] += lax.dot_general(pb, vb_ref[n, h // HEADS_PER_TILE], NT_DIMS,
                                          preferred_element_type=F32)

    over_past_blocks(values)

    for p in range(n_pairs):
        o = [acc_ref[p * HEADS_PER_TILE + hh]
             / jnp.sum(l_ref[p * HEADS_PER_TILE + hh], axis=-1, keepdims=True)
             for hh in range(HEADS_PER_TILE)]
        o_ref[0, :, p * LANES:(p + 1) * LANES] = jnp.where(lane < HEAD_DIM, o[0], o[1])


def _attn_prompt(q, kt, vt, layer):
    b, t, _ = q.shape
    assert t % MOBA_BLOCK == 0
    nb = t // MOBA_BLOCK
    n_pairs = D_ATTN // LANES
    assert nb <= SUBLANES and N_HEADS * SUBLANES <= LANES
    qspec = pl.BlockSpec((1, MOBA_BLOCK, D_ATTN), lambda bi, i: (bi, i, 0))
    kspec = pl.BlockSpec((1, None, D_ATTN, t), lambda bi, i: (bi, layer, 0, 0))
    head_acc = pltpu.VMEM((N_HEADS, MOBA_BLOCK, LANES), F32)
    return pl.pallas_call(
        functools.partial(_attn_prompt_kernel, k_top=min(MOBA_TOP_K, nb)),
        grid=(b, nb),
        in_specs=[qspec, kspec, kspec],
        out_specs=qspec,
        out_shape=jax.ShapeDtypeStruct((b, t, D_ATTN), F32),
        scratch_shapes=[pltpu.VMEM((nb, n_pairs, 2 * LANES, MOBA_BLOCK), BF16),
                        pltpu.VMEM((nb, n_pairs, LANES, MOBA_BLOCK), BF16),
                        pltpu.VMEM((2, N_HEADS * SUBLANES, D_ATTN), BF16),
                        pltpu.VMEM((N_HEADS, MOBA_BLOCK, 2 * LANES), BF16),
                        pltpu.VMEM((N_HEADS, nb, MOBA_BLOCK, MOBA_BLOCK), F32),
                        head_acc, head_acc, head_acc],
        compiler_params=_params("parallel", "arbitrary"),
    )(q, kt, vt)


def _attn_sample_kernel(pt_ref, q_ref, kn_ref, vn_ref, *refs, n_pages, k_top):
    del pt_ref
    o_ref = refs[-1]
    for s in range(q_ref.shape[0]):
        kp = refs[2 * s * n_pages:(2 * s + 1) * n_pages]
        vp = refs[(2 * s + 1) * n_pages:(2 * s + 2) * n_pages]
        o_ref[s] = _attend_sample(q_ref[s], kn_ref[s], vn_ref[s], kp, vp, k_top)


def _attend_sample(q, kn, vn, kp, vp, k_top):
    n_pages = len(kp)
    ts = q.shape[0]
    ppb = MOBA_BLOCK // kp[0].shape[3]
    nb = n_pages // ppb
    rows = N_HEADS * ts
    block = lambda pages, n: jnp.concatenate([pages[n * ppb + j][0, 0] for j in range(ppb)], axis=1)

    lane_head = lax.broadcasted_iota(jnp.int32, (ts, D_ATTN), 1) // HEAD_DIM
    q_bd = jnp.concatenate([jnp.where(lane_head == h, q, 0.0) for h in range(N_HEADS)], axis=0)
    qs = (q_bd * ATTN_SCALE).astype(BF16)

    lane = lax.broadcasted_iota(jnp.int32, (D_ATTN, LANES), 1)
    km_t = jnp.zeros((D_ATTN, LANES), F32)
    s_list = []
    for n in range(nb):
        kt = block(kp, n)
        col = jnp.sum(kt, axis=1, keepdims=True) * (1.0 / MOBA_BLOCK)
        km_t = jnp.where(lane == n, col, km_t)
        s_list.append(jnp.dot(qs, kt.astype(BF16), preferred_element_type=F32))

    gate = jnp.dot(q_bd, km_t, precision=lax.Precision.HIGHEST, preferred_element_type=F32)
    blk = lax.broadcasted_iota(jnp.int32, (rows, LANES), 1)
    gm = jnp.where(blk < nb, gate, -jnp.inf)
    rank = jnp.zeros((rows, LANES), jnp.int32)
    for m in range(nb):
        gc = gm[:, m:m + 1]
        beats = (gc > gm) | ((gc == gm) & (m < blk))
        rank = rank + beats.astype(jnp.int32)
    sel = (rank < k_top) & (gm > -jnp.inf) & (gm < jnp.inf)
    bias = jnp.where(sel, 0.0, MASK_NEG)

    pad = jnp.zeros((LANES - ts, D_ATTN), F32)
    k_new = jnp.concatenate([kn, pad], axis=0).astype(BF16)
    v_new = jnp.concatenate([vn, pad], axis=0).astype(BF16)
    s_new = lax.dot_general(qs, k_new, NT_DIMS, preferred_element_type=F32)
    qi = lax.broadcasted_iota(jnp.int32, (rows, LANES), 0) % ts
    s_new = jnp.where(blk <= qi, s_new, MASK_NEG)

    mx = jnp.max(s_new, axis=-1, keepdims=True)
    for n in range(nb):
        s_list[n] = s_list[n] + bias[:, n:n + 1]
        mx = jnp.maximum(mx, jnp.max(s_list[n], axis=-1, keepdims=True))
    p_new = jnp.exp(s_new - mx)
    den = jnp.sum(p_new, axis=-1, keepdims=True)
    acc = jnp.dot(p_new.astype(BF16), v_new, preferred_element_type=F32)
    for n in range(nb):
        p_n = jnp.exp(s_list[n] - mx)
        den = den + jnp.sum(p_n, axis=-1, keepdims=True)
        vt = block(vp, n).astype(BF16)
        acc = acc + lax.dot_general(p_n.astype(BF16), vt, NT_DIMS, preferred_element_type=F32)
    acc = acc / den
    out = jnp.zeros((ts, D_ATTN), F32)
    for h in range(N_HEADS):
        out = out + jnp.where(lane_head == h, acc[h * ts:(h + 1) * ts, :], 0.0)
    return out


def _attn_sample(q, k_new, v_new, cache_kt, cache_vt, page_table, layer):
    b, ts, _ = q.shape
    n_pages = page_table.shape[1]
    page = cache_kt.shape[3]
    past_len = n_pages * page
    assert ts == SUBLANES and MOBA_BLOCK % page == 0 and past_len % MOBA_BLOCK == 0
    nb = past_len // MOBA_BLOCK
    assert nb <= LANES
    spb = SAMPLES_PER_STEP if b % SAMPLES_PER_STEP == 0 else 1
    tok = pl.BlockSpec((spb, ts, D_ATTN), lambda bi, pt: (bi, 0, 0))

    def page_spec(s, j):
        return pl.BlockSpec((1, 1, D_ATTN, page), lambda bi, pt: (pt[bi * spb + s, j], layer, 0, 0))

    pages, caches = [], []
    for s in range(spb):
        pages += [page_spec(s, j) for j in range(n_pages)] * 2
        caches += [cache_kt] * n_pages + [cache_vt] * n_pages
    return pl.pallas_call(
        functools.partial(_attn_sample_kernel, n_pages=n_pages, k_top=min(MOBA_TOP_K, nb)),
        grid_spec=pltpu.PrefetchScalarGridSpec(
            num_scalar_prefetch=1,
            grid=(b // spb,),
            in_specs=[tok, tok, tok] + pages,
            out_specs=tok,
        ),
        out_shape=jax.ShapeDtypeStruct((b, ts, D_ATTN), F32),
        compiler_params=_params("parallel"),
    )(page_table, q, k_new, v_new, *caches)


def _lane_windows(shape):
    grp = lax.broadcasted_iota(jnp.int32, shape, len(shape) - 1) // (D_POOL // POOL_GROUPS)
    win = jnp.full(shape, POOL_WINDOWS[0], jnp.int32)
    for gi in range(1, POOL_GROUPS):
        win = jnp.where(grp == gi, POOL_WINDOWS[gi], win)
    return win


def _prompt_branches(ti, ug_ref, vn_ref, xc_ref, prev_ref, ws_ref, bs_ref, wp_ref, ps_ref):
    tm = ug_ref.shape[0]
    dg = D_GMLP // GMLP_GROUPS

    wr = lax.broadcasted_iota(jnp.int32, (GMLP_CHUNK, GMLP_GROUPS * GMLP_CHUNK), 0)
    wc = lax.broadcasted_iota(jnp.int32, (GMLP_CHUNK, GMLP_GROUPS * GMLP_CHUNK), 1) % GMLP_CHUNK
    w_tril = jnp.where(wc <= wr, ws_ref[...], 0.0).astype(BF16)
    lane_grp = lax.broadcasted_iota(jnp.int32, (GMLP_CHUNK, D_GMLP), 1) // dg
    b_chunks = []
    for c in range(tm // GMLP_CHUNK):
        rows = slice(c * GMLP_CHUNK, (c + 1) * GMLP_CHUNK)
        vc = vn_ref[rows, :]
        v_stack = jnp.concatenate([jnp.where(lane_grp == gi, vc, 0.0) for gi in range(GMLP_GROUPS)],
                                  axis=0).astype(BF16)
        mixed = jnp.dot(w_tril, v_stack, preferred_element_type=F32) + bs_ref[...]
        b_chunks.append(ug_ref[rows, :] * mixed)
    b_out = jnp.concatenate(b_chunks, axis=0)

    x = xc_ref[...]
    prev = jnp.where(ti > 0, prev_ref[...], 0.0)
    xe = jnp.concatenate([prev, x], axis=0)
    s2 = xe + pltpu.roll(xe, 1, axis=0)
    s4 = s2 + pltpu.roll(s2, 2, axis=0)
    s8 = s4 + pltpu.roll(s4, 4, axis=0)
    s16 = s8 + pltpu.roll(s8, 8, axis=0)
    win = _lane_windows((tm, D_POOL))
    pre = POOL_PREV_ROWS
    wsum = jnp.where(win == 2, s2[pre:], jnp.where(win == 4, s4[pre:],
                     jnp.where(win == 8, s8[pre:], s16[pre:])))
    pos = ti * tm + lax.broadcasted_iota(jnp.int32, (tm, D_POOL), 0)
    count = jnp.minimum(win, pos + 1).astype(F32)
    pooled = wsum / count - x
    y = jnp.dot(pooled.astype(BF16), wp_ref[...], preferred_element_type=F32)
    return b_out, y * ps_ref[...]


def _branch_sample_kernel(ug_ref, vn_ref, xc_ref, pre_ref, wc_ref, bs_ref, wp_ref, ps_ref,
                          b_ref, c_ref, *, pos0):
    nb, ts, _ = ug_ref.shape
    vn = vn_ref[...]
    t_idx = lax.broadcasted_iota(jnp.int32, (ts, D_GMLP), 0)
    mixed = jnp.zeros((nb, ts, D_GMLP), F32) + bs_ref[...][None]
    for s in range(ts):
        coef = jnp.where(s <= t_idx, wc_ref[s], 0.0)
        mixed = mixed + coef[None] * vn[:, s:s + 1, :]
    b_ref[...] = ug_ref[...] * mixed

    x = xc_ref[...]
    pre = pre_ref[...]
    win = _lane_windows((ts, D_POOL))
    end = POOL_BUF + t_idx
    wsum = jnp.zeros((nb, ts, D_POOL), F32)
    for r in range(POOL_BUF + ts):
        row = pre[:, r:r + 1, :] if r < POOL_BUF else x[:, r - POOL_BUF:r - POOL_BUF + 1, :]
        inside = (r <= end) & (r > end - win)
        wsum = wsum + jnp.where(inside, 1.0, 0.0)[None] * row
    count = jnp.minimum(win, pos0 + t_idx + 1).astype(F32)
    pooled = wsum / count[None] - x
    y = jnp.dot(pooled.reshape(nb * ts, D_POOL).astype(BF16), wp_ref[...], preferred_element_type=F32)
    c_ref[...] = (y * ps_ref[...]).reshape(nb, ts, D_POOL)


def _branch_sample(ug, vn, xc, prefix, wc, bs_t, wp_bd, ps, layer, pos0):
    b, ts, _ = ug.shape
    assert ts == SUBLANES
    full = lambda shape: pl.BlockSpec(shape, lambda i: (0,) * len(shape))
    return pl.pallas_call(
        functools.partial(_branch_sample_kernel, pos0=pos0),
        grid=(1,),
        in_specs=[full(ug.shape), full(vn.shape), full(xc.shape), full(prefix.shape)]
        + [_layer_resident(w.shape, layer) for w in (wc, bs_t, wp_bd, ps)],
        out_specs=[full(ug.shape), full(xc.shape)],
        out_shape=[jax.ShapeDtypeStruct(ug.shape, F32), jax.ShapeDtypeStruct(xc.shape, F32)],
        compiler_params=_params("arbitrary"),
    )(ug, vn, xc, prefix, wc, bs_t, wp_bd, ps)


def _merge_mlp_kernel(h_ref, a_ref, *refs, final, ff_chunk, n_branch_refs):
    branch_refs = refs[:n_branch_refs]
    gates_ref, wa_ref, wb_ref, wc_ref, wo_ref, g_ref, wu_ref, wd_ref, gf_ref, o_ref = refs[n_branch_refs:]
    if n_branch_refs == 2:
        b_out, c_out = branch_refs[0][...], branch_refs[1][...]
    else:
        b_out, c_out = _prompt_branches(pl.program_id(1), *branch_refs)

    def branch(x, w_ref, j):
        y = jnp.dot(x.astype(BF16), w_ref[...], preferred_element_type=F32)
        return gates_ref[:, j * D_MODEL:(j + 1) * D_MODEL].astype(F32) * y

    merged = branch(a_ref[...], wa_ref, 0) + branch(b_out, wb_ref, 1) + branch(c_out, wc_ref, 2)
    h1 = h_ref[...] + jnp.dot(merged.astype(BF16), wo_ref[...], preferred_element_type=F32)
    xb = _rms(h1, g_ref[...]).astype(BF16)
    acc = h1
    for c in range(D_FF // ff_chunk):
        cols = slice(c * ff_chunk, (c + 1) * ff_chunk)
        up = jnp.dot(xb, wu_ref[:, cols], preferred_element_type=F32)
        act = jnp.square(jnp.maximum(up, 0.0)).astype(BF16)
        acc = acc + jnp.dot(act, wd_ref[cols, :], preferred_element_type=F32)
    o_ref[...] = _rms(acc, gf_ref[...]) if final else acc


def _merge_mlp(h, a, branches, gates, wa, wb, wc, wo, g, wu, wd, gf, layer, final, nseq, tm):
    n = h.shape[0]
    nt = n // nseq // tm
    row = lambda w: pl.BlockSpec((tm, w), lambda bi, ti: (bi * nt + ti, 0))
    stack = lambda w: _layer_resident(w.shape, layer)
    if len(branches) == 2:
        branch_specs, branch_ops = [row(D_GMLP), row(D_POOL)], list(branches)
    else:
        ug, vn, xc, ws_cat, bs_t, wp_bd, ps = branches
        assert tm % GMLP_CHUNK == 0 and tm % POOL_PREV_ROWS == 0
        per = tm // POOL_PREV_ROWS
        prev = pl.BlockSpec((POOL_PREV_ROWS, D_POOL),
                            lambda bi, ti: (jnp.maximum((bi * nt + ti) * per - 1, 0), 0))
        branch_specs = [row(D_GMLP), row(D_GMLP), row(D_POOL), prev] + [
            _layer_resident(w.shape, layer) for w in (ws_cat, bs_t, wp_bd, ps)]
        branch_ops = [ug, vn, xc, xc, ws_cat, bs_t, wp_bd, ps]
    return pl.pallas_call(
        functools.partial(_merge_mlp_kernel, final=final, ff_chunk=1024,
                          n_branch_refs=len(branch_ops)),
        grid=(nseq, nt),
        in_specs=[row(D_MODEL), row(D_ATTN)] + branch_specs + [
            row(3 * D_MODEL), stack(wa), stack(wb), stack(wc), stack(wo), stack(g),
            stack(wu), stack(wd), _resident((1, D_MODEL))],
        out_specs=row(D_MODEL),
        out_shape=jax.ShapeDtypeStruct((n, D_MODEL), F32),
        compiler_params=_params("parallel", "parallel"),
    )(h, a, *branch_ops, gates, wa, wb, wc, wo, g, wu, wd, gf)


def _block_diag(w):
    n, g, d, e = w.shape
    eye = jnp.eye(g, dtype=w.dtype)
    return (eye[None, :, None, :, None] * w[:, :, :, None, :]).reshape(n, g * d, g * e)


def _pages_channel_major(cache):
    n_phys, depth, page, _, _ = cache.shape
    return jnp.transpose(cache, (0, 1, 3, 4, 2)).reshape(n_phys, depth, D_ATTN, page)


def _heads_last(kt):
    b, depth, _, t = kt.shape
    return jnp.transpose(kt.reshape(b, depth, N_HEADS, HEAD_DIM, t), (0, 1, 4, 2, 3))


def kernel(x_prompt, x_sample, cache_k, cache_v, state_pool, page_table, norm1_g, w_in, gmlp_norm_g,
           w_spatial, b_spatial, w_pool, pool_scale, w_branch_a, w_branch_b, w_branch_c, w_out,
           norm2_g, w_up, w_down, final_norm_g):
    bp, tp, _ = x_prompt.shape
    bs, ts, _ = x_sample.shape
    depth = w_in.shape[0]
    past_len = page_table.shape[1] * cache_k.shape[2]
    dg = D_GMLP // GMLP_GROUPS
    tm = min(512, tp)
    tms = min(512, bs * ts)

    ckt = _pages_channel_major(cache_k)
    cvt = _pages_channel_major(cache_v)
    w_in_b = w_in.astype(BF16)
    wkv_t = _kv_weights_transposed(w_in)
    wa_b, wb_b, wc_b = w_branch_a.astype(BF16), w_branch_b.astype(BF16), w_branch_c.astype(BF16)
    wo_b, wu_b, wd_b = w_out.astype(BF16), w_up.astype(BF16), w_down.astype(BF16)
    gf = final_norm_g.reshape(1, D_MODEL)
    g1 = norm1_g.reshape(depth, 1, D_MODEL)
    g2 = norm2_g.reshape(depth, 1, D_MODEL)
    gn = gmlp_norm_g.reshape(depth, 1, D_GMLP)
    ps = pool_scale.reshape(depth, 1, D_POOL)
    wp_bd = _block_diag(w_pool).astype(BF16)
    ws_cat = jnp.transpose(w_spatial, (0, 2, 1, 3)).reshape(depth, GMLP_CHUNK, GMLP_GROUPS * GMLP_CHUNK)
    bs_full = jnp.repeat(jnp.transpose(b_spatial, (0, 2, 1)), dg, axis=2)
    wc_s = jnp.repeat(jnp.transpose(w_spatial[:, :, :ts, :ts], (0, 3, 2, 1)), dg, axis=3)
    bs_s = bs_full[:, :ts]

    hp = x_prompt.reshape(bp * tp, D_MODEL)
    hs = x_sample.reshape(bs * ts, D_MODEL)
    outs = {n: [] for n in ("pp", "ks", "vs", "ps", "gv")}
    kv_stacks = (jnp.zeros((bp, depth, D_ATTN, tp), F32), jnp.zeros((bp, depth, D_ATTN, tp), F32))
    for l in range(depth):
        q, kst, vst, ug, vn, xc, gates = _proj(hp, g1, w_in_b, wkv_t, gn, l, bp, tm, kv_stacks)
        kv_stacks = (kst, vst)
        r3 = lambda a: a.reshape(bp, tp, a.shape[-1])
        a_out = _attn_prompt(r3(q), kst, vst, l)
        hp = _merge_mlp(hp, a_out.reshape(bp * tp, D_ATTN), (ug, vn, xc, ws_cat, bs_full, wp_bd, ps),
                        gates, wa_b, wb_b, wc_b, wo_b, g2, wu_b, wd_b, gf, l, l == depth - 1, bp, tm)
        outs["pp"].append(r3(xc)[:, tp - POOL_BUF:])

        q, k, v, ug, vn, xc, gates = _proj(hs, g1, w_in_b, wkv_t, gn, l, 1, tms)
        s3 = lambda a: a.reshape(bs, ts, a.shape[-1])
        a_out = _attn_sample(s3(q), s3(k), s3(v), ckt, cvt, page_table, l)
        prefix = state_pool[:, l]
        b_out, c_out = _branch_sample(s3(ug), s3(vn), s3(xc), prefix, wc_s, bs_s, wp_bd, ps, l, past_len)
        hs = _merge_mlp(hs, a_out.reshape(bs * ts, D_ATTN),
                        (b_out.reshape(bs * ts, D_GMLP), c_out.reshape(bs * ts, D_POOL)),
                        gates, wa_b, wb_b, wc_b, wo_b, g2, wu_b, wd_b, gf, l, l == depth - 1, 1, tms)
        outs["ks"].append(k.reshape(bs, ts, N_HEADS, HEAD_DIM))
        outs["vs"].append(v.reshape(bs, ts, N_HEADS, HEAD_DIM))
        outs["ps"].append(jnp.concatenate([prefix, s3(xc)], axis=1)[:, -POOL_BUF:])
        outs["gv"].append(s3(vn))

    stack = lambda n: jnp.stack(outs[n], axis=1)
    return (hp.reshape(bp, tp, D_MODEL), hs.reshape(bs, ts, D_MODEL), _heads_last(kv_stacks[0]),
            _heads_last(kv_stacks[1]), stack("pp"), stack("ks"), stack("vs"), stack("ps"), stack("gv"))
```

```python
import functools

import jax
import jax.numpy as jnp
from jax import lax
from jax.experimental import pallas as pl
from jax.experimental.pallas import tpu as pltpu

F32 = jnp.float32
BF16 = jnp.bfloat16

D_MODEL = 1024
N_HEADS = 8
HEAD_DIM = 64
D_ATTN = N_HEADS * HEAD_DIM
MOBA_BLOCK = 256
MOBA_TOP_K = 3
GMLP_GROUPS = 4
D_GMLP = 256
GMLP_CHUNK = 128
POOL_GROUPS = 4
D_POOL = 256
POOL_WINDOWS = (2, 4, 8, 16)
POOL_BUF = max(POOL_WINDOWS) - 1
D_FF = 4 * D_MODEL
NORM_EPS = 1e-6
N_IN = 3 * D_ATTN + 2 * D_GMLP + D_POOL + 3 * D_MODEL
OFF_Q, OFF_K, OFF_V = 0, D_ATTN, 2 * D_ATTN
OFF_U = 3 * D_ATTN
OFF_VG = OFF_U + D_GMLP
OFF_XC = OFF_VG + D_GMLP
OFF_GATES = OFF_XC + D_POOL

LANES = 128
SUBLANES = 8
HEADS_PER_TILE = LANES // HEAD_DIM
ATTN_SCALE = HEAD_DIM ** -0.5
LOG2E = 1.4426950408889634
MASK_NEG = -(2.0 ** 100)
VMEM_LIMIT = 56 * 1024 * 1024
POOL_PREV_ROWS = 2 * SUBLANES
SAMPLES_PER_STEP = 2

NT_DIMS = (((1,), (1,)), ((), ()))


def _params(*sem):
    return pltpu.CompilerParams(dimension_semantics=sem, vmem_limit_bytes=VMEM_LIMIT)


def _resident(shape):
    nd = len(shape)
    return pl.BlockSpec(shape, lambda *_: (0,) * nd, pipeline_mode=pl.Buffered(1))


def _layer_resident(stacked_shape, layer):
    nd = len(stacked_shape) - 1
    return pl.BlockSpec((None,) + tuple(stacked_shape[1:]), lambda *_: (layer,) + (0,) * nd,
                        pipeline_mode=pl.Buffered(1))


def _split_bf16(x):
    hi = x.astype(BF16)
    return hi, (x - hi.astype(F32)).astype(BF16)


def _rms(x, g):
    return x * lax.rsqrt(jnp.mean(x * x, axis=-1, keepdims=True) + NORM_EPS) * g


def _proj_kernel(h_ref, g_ref, w_ref, wkv_t_ref, gn_ref, *refs, kv_transposed):
    q_ref, k_ref, v_ref, ug_ref, vn_ref, xc_ref, gates_ref = refs[-7:]
    xb = _rms(h_ref[...], g_ref[...]).astype(BF16)

    def seg(lo, width):
        return jnp.dot(xb, w_ref[:, lo:lo + width], preferred_element_type=F32)

    q_ref[...] = seg(OFF_Q, D_ATTN)
    if kv_transposed:
        k_ref[0, 0] = lax.dot_general(wkv_t_ref[0:D_ATTN, :], xb, NT_DIMS,
                                      preferred_element_type=F32)
        v_ref[0, 0] = lax.dot_general(wkv_t_ref[D_ATTN:2 * D_ATTN, :], xb, NT_DIMS,
                                      preferred_element_type=F32)
    else:
        k_ref[...] = seg(OFF_K, D_ATTN)
        v_ref[...] = seg(OFF_V, D_ATTN)
    ug_ref[...] = jax.nn.gelu(seg(OFF_U, D_GMLP))
    gv = jax.nn.gelu(seg(OFF_VG, D_GMLP))
    gc = gv - jnp.mean(gv, axis=-1, keepdims=True)
    vn_ref[...] = gc * lax.rsqrt(jnp.mean(gc * gc, axis=-1, keepdims=True) + NORM_EPS) * gn_ref[...]
    xc_ref[...] = seg(OFF_XC, D_POOL)
    for j in range(3):
        z = seg(OFF_GATES + j * D_MODEL, D_MODEL)
        gates_ref[:, j * D_MODEL:(j + 1) * D_MODEL] = jax.nn.sigmoid(z).astype(BF16)


def _proj(h, g, w_in, wkv_t, gn, layer, nseq, tm, kv_stacks=None):
    n = h.shape[0]
    t = n // nseq
    nt = t // tm
    row = lambda w: pl.BlockSpec((tm, w), lambda bi, ti: (bi * nt + ti, 0))
    flat = lambda w, dt=F32: jax.ShapeDtypeStruct((n, w), dt)
    in_specs = [row(D_MODEL)] + [_layer_resident(w.shape, layer) for w in (g, w_in, wkv_t, gn)]
    operands = [h, g, w_in, wkv_t, gn]
    if kv_stacks is not None:
        kv_spec = pl.BlockSpec((1, 1, D_ATTN, tm), lambda bi, ti: (bi, layer, 0, ti))
        kv_shape = jax.ShapeDtypeStruct(kv_stacks[0].shape, F32)
        aliases = {}
        if not isinstance(kv_stacks[0], jax.ShapeDtypeStruct):
            aliases = {len(operands): 1, len(operands) + 1: 2}
            in_specs += [pl.BlockSpec(memory_space=pl.ANY)] * 2
            operands += list(kv_stacks)
    else:
        kv_spec, kv_shape, aliases = row(D_ATTN), flat(D_ATTN), {}
    return pl.pallas_call(
        functools.partial(_proj_kernel, kv_transposed=kv_stacks is not None),
        grid=(nseq, nt),
        in_specs=in_specs,
        out_specs=[row(D_ATTN), kv_spec, kv_spec, row(D_GMLP), row(D_GMLP), row(D_POOL),
                   row(3 * D_MODEL)],
        out_shape=[flat(D_ATTN), kv_shape, kv_shape, flat(D_GMLP), flat(D_GMLP), flat(D_POOL),
                   flat(3 * D_MODEL, BF16)],
        input_output_aliases=aliases,
        compiler_params=_params("parallel", "parallel"),
    )(*operands)


def _transpose_cast_kernel(w_ref, o_ref):
    o_ref[0] = w_ref[0].T.astype(BF16)


def _kv_weights_transposed(w_in):
    depth = w_in.shape[0]
    assert OFF_K % D_ATTN == 0 and OFF_V == OFF_K + D_ATTN
    return pl.pallas_call(
        _transpose_cast_kernel,
        grid=(depth, 2),
        in_specs=[pl.BlockSpec((1, D_MODEL, D_ATTN), lambda l, j: (l, 0, OFF_K // D_ATTN + j))],
        out_specs=pl.BlockSpec((1, D_ATTN, D_MODEL), lambda l, j: (l, j, 0)),
        out_shape=jax.ShapeDtypeStruct((depth, 2 * D_ATTN, D_MODEL), BF16),
        compiler_params=_params("parallel", "parallel"),
    )(w_in)


def _attn_prompt_kernel(q_ref, kt_ref, vt_ref, o_ref, kaug_ref, vb_ref, kbd_ref, qaug_ref, s_ref,
                        mx_ref, l_ref, acc_ref, *, k_top):
    i = pl.program_id(1)
    nb, n_pairs = kaug_ref.shape[0], kaug_ref.shape[1]
    nbp = SUBLANES
    fold = lambda x: (x[:, 0:LANES], x[:, LANES:2 * LANES])

    @pl.when(i == 0)
    def _():
        lane = lax.broadcasted_iota(jnp.int32, (LANES, LANES), 1)
        oh_row = lax.broadcasted_iota(jnp.int32, (LANES, MOBA_BLOCK), 0)
        half = lax.broadcasted_iota(jnp.int32, (nbp, LANES), 1) // HEAD_DIM
        kbd_ref[...] = jnp.zeros(kbd_ref.shape, BF16)
        for p in range(n_pairs):
            chans = slice(p * LANES, (p + 1) * LANES)
            km_t = jnp.zeros((LANES, LANES), F32)
            for n in range(nb):
                cols = slice(n * MOBA_BLOCK, (n + 1) * MOBA_BLOCK)
                ktn = kt_ref[0, chans, cols]
                kaug_ref[n, p, 0:LANES, :] = ktn.astype(BF16)
                onehot = (oh_row % nbp == n) & (oh_row // (HEADS_PER_TILE * nbp) == p)
                kaug_ref[n, p, LANES:2 * LANES, :] = jnp.where(onehot, 1.0, 0.0).astype(BF16)
                vb_ref[n, p] = vt_ref[0, chans, cols].astype(BF16)
                col = jnp.sum(ktn, axis=1, keepdims=True) * (1.0 / MOBA_BLOCK)
                km_t = jnp.where(lane == n, col, km_t)
            km = km_t.T
            km = jnp.concatenate([jnp.where(half == hh, km[0:nbp], 0.0)
                                  for hh in range(HEADS_PER_TILE)], axis=0)
            rows = slice(p * HEADS_PER_TILE * nbp, (p + 1) * HEADS_PER_TILE * nbp)
            km_hi, km_lo = _split_bf16(km)
            kbd_ref[0, rows, chans] = km_hi
            kbd_ref[1, rows, chans] = km_lo

    qf = q_ref[0]
    q_hi, q_lo = _split_bf16(qf)
    nt_dot = lambda a, b: lax.dot_general(a, b, NT_DIMS, preferred_element_type=F32)
    g = nt_dot(kbd_ref[0], q_hi) + (nt_dot(kbd_ref[0], q_lo) + nt_dot(kbd_ref[1], q_hi))
    gm = g.reshape(N_HEADS, nbp, MOBA_BLOCK)
    blk = lax.broadcasted_iota(jnp.int32, gm.shape, 1)
    past = blk < i
    gm = jnp.where(past, gm, -jnp.inf)
    rank = jnp.zeros(gm.shape, jnp.int32)
    for m in range(nb):
        gr = gm[:, m:m + 1, :]
        beats = (gr > gm) | ((gr == gm) & (m < blk))
        rank = rank + beats.astype(jnp.int32)
    sel = past & (rank < k_top) & (gm > -jnp.inf) & (gm < jnp.inf)
    sel_t = jnp.where(sel, 1.0, 0.0).reshape(N_HEADS * nbp, MOBA_BLOCK)
    sel_t = jnp.concatenate([sel_t, jnp.zeros((LANES - N_HEADS * nbp, MOBA_BLOCK), F32)], axis=0)
    rr = lax.broadcasted_iota(jnp.int32, (MOBA_BLOCK, MOBA_BLOCK), 0)
    cc = lax.broadcasted_iota(jnp.int32, (MOBA_BLOCK, MOBA_BLOCK), 1)
    eye = jnp.where(rr == cc, 1.0, 0.0).astype(BF16)
    sel_n = lax.dot_general(eye, sel_t.astype(BF16), NT_DIMS, preferred_element_type=F32)
    bias_all = jnp.where(sel_n > 0.5, 0.0, MASK_NEG)
    lane = lax.broadcasted_iota(jnp.int32, (MOBA_BLOCK, LANES), 1)
    for h in range(N_HEADS):
        p, hh = divmod(h, HEADS_PER_TILE)
        qh = jnp.where(lane // HEAD_DIM == hh, qf[:, p * LANES:(p + 1) * LANES], 0.0)
        qaug_ref[h, :, 0:LANES] = (qh * (ATTN_SCALE * LOG2E)).astype(BF16)
        qaug_ref[h, :, LANES:2 * LANES] = jnp.where(lane // nbp == h, bias_all, 0.0).astype(BF16)

    mx_ref[...] = jnp.full(mx_ref.shape, MASK_NEG, F32)

    def over_past_blocks(block_fn):
        def quad(j, carry):
            for d in range(4):
                block_fn(4 * j + d)
            return carry

        lax.fori_loop(0, i // 4, quad, 0)
        rem = i % 4

        @pl.when(rem >= 2)
        def _():
            block_fn(i - rem)
            block_fn(i - rem + 1)

        @pl.when(rem % 2 == 1)
        def _():
            block_fn(i - 1)

    def scores(n):
        for h in range(N_HEADS):
            s = jnp.dot(qaug_ref[h], kaug_ref[n, h // HEADS_PER_TILE],
                        preferred_element_type=F32)
            s_ref[h, n] = s
            lo, hi = fold(s)
            mx_ref[h] = jnp.maximum(mx_ref[h], jnp.maximum(lo, hi))

    over_past_blocks(scores)

    causal = cc <= rr
    for h in range(N_HEADS):
        p = h // HEADS_PER_TILE
        s_own = jnp.dot(qaug_ref[h, :, 0:LANES], kaug_ref[i, p, 0:LANES, :],
                        preferred_element_type=F32)
        s_own = jnp.where(causal, s_own, MASK_NEG)
        lo, hi = fold(s_own)
        mx = jnp.max(jnp.maximum(mx_ref[h], jnp.maximum(lo, hi)), axis=-1, keepdims=True)
        mx_ref[h] = jnp.broadcast_to(mx, (MOBA_BLOCK, LANES))
        p_own = jnp.exp2(s_own - mx)
        lo, hi = fold(p_own)
        l_ref[h] = lo + hi
        acc_ref[h] = lax.dot_general(p_own.astype(BF16), vb_ref[i, p], NT_DIMS,
                                     preferred_element_type=F32)

    def values(n):
        for h in range(N_HEADS):
            mx = mx_ref[h]
            lo, hi = fold(s_ref[h, n])
            p_lo = jnp.exp2(lo - mx)
            p_hi = jnp.exp2(hi - mx)
            l_ref[h] += p_lo + p_hi
            pb = jnp.concatenate([p_lo, p_hi], axis=1).astype(BF16)
            acc_ref[h] += lax.dot_general(pb, vb_ref[n, h // HEADS_PER_TILE], NT_DIMS,
                                          preferred_element_type=F32)

    over_past_blocks(values)

    for p in range(n_pairs):
        o = [acc_ref[p * HEADS_PER_TILE + hh]
             / jnp.sum(l_ref[p * HEADS_PER_TILE + hh], axis=-1, keepdims=True)
             for hh in range(HEADS_PER_TILE)]
        o_ref[0, :, p * LANES:(p + 1) * LANES] = jnp.where(lane < HEAD_DIM, o[0], o[1])


def _attn_prompt(q, kt, vt, layer):
    b, t, _ = q.shape
    assert t % MOBA_BLOCK == 0
    nb = t // MOBA_BLOCK
    n_pairs = D_ATTN // LANES
    assert nb <= SUBLANES and N_HEADS * SUBLANES <= LANES
    qspec = pl.BlockSpec((1, MOBA_BLOCK, D_ATTN), lambda bi, i: (bi, i, 0))
    kspec = pl.BlockSpec((1, None, D_ATTN, t), lambda bi, i: (bi, layer, 0, 0))
    head_acc = pltpu.VMEM((N_HEADS, MOBA_BLOCK, LANES), F32)
    return pl.pallas_call(
        functools.partial(_attn_prompt_kernel, k_top=min(MOBA_TOP_K, nb)),
        grid=(b, nb),
        in_specs=[qspec, kspec, kspec],
        out_specs=qspec,
        out_shape=jax.ShapeDtypeStruct((b, t, D_ATTN), F32),
        scratch_shapes=[pltpu.VMEM((nb, n_pairs, 2 * LANES, MOBA_BLOCK), BF16),
                        pltpu.VMEM((nb, n_pairs, LANES, MOBA_BLOCK), BF16),
                        pltpu.VMEM((2, N_HEADS * SUBLANES, D_ATTN), BF16),
                        pltpu.VMEM((N_HEADS, MOBA_BLOCK, 2 * LANES), BF16),
                        pltpu.VMEM((N_HEADS, nb, MOBA_BLOCK, MOBA_BLOCK), F32),
                        head_acc, head_acc, head_acc],
        compiler_params=_params("parallel", "arbitrary"),
    )(q, kt, vt)


def _attn_sample_kernel(pt_ref, q_ref, kn_ref, vn_ref, *refs, n_pages, k_top):
    del pt_ref
    o_ref = refs[-1]
    for s in range(q_ref.shape[0]):
        kp = refs[2 * s * n_pages:(2 * s + 1) * n_pages]
        vp = refs[(2 * s + 1) * n_pages:(2 * s + 2) * n_pages]
        o_ref[s] = _attend_sample(q_ref[s], kn_ref[s], vn_ref[s], kp, vp, k_top)


def _attend_sample(q, kn, vn, kp, vp, k_top):
    n_pages = len(kp)
    ts = q.shape[0]
    ppb = MOBA_BLOCK // kp[0].shape[3]
    nb = n_pages // ppb
    rows = N_HEADS * ts
    block = lambda pages, n: jnp.concatenate([pages[n * ppb + j][0, 0] for j in range(ppb)], axis=1)

    lane_head = lax.broadcasted_iota(jnp.int32, (ts, D_ATTN), 1) // HEAD_DIM
    q_bd = jnp.concatenate([jnp.where(lane_head == h, q, 0.0) for h in range(N_HEADS)], axis=0)
    qs = (q_bd * ATTN_SCALE).astype(BF16)

    lane = lax.broadcasted_iota(jnp.int32, (D_ATTN, LANES), 1)
    km_t = jnp.zeros((D_ATTN, LANES), F32)
    s_list = []
    for n in range(nb):
        kt = block(kp, n)
        col = jnp.sum(kt, axis=1, keepdims=True) * (1.0 / MOBA_BLOCK)
        km_t = jnp.where(lane == n, col, km_t)
        s_list.append(jnp.dot(qs, kt.astype(BF16), preferred_element_type=F32))

    gate = jnp.dot(q_bd, km_t, precision=lax.Precision.HIGHEST, preferred_element_type=F32)
    blk = lax.broadcasted_iota(jnp.int32, (rows, LANES), 1)
    gm = jnp.where(blk < nb, gate, -jnp.inf)
    rank = jnp.zeros((rows, LANES), jnp.int32)
    for m in range(nb):
        gc = gm[:, m:m + 1]
        beats = (gc > gm) | ((gc == gm) & (m < blk))
        rank = rank + beats.astype(jnp.int32)
    sel = (rank < k_top) & (gm > -jnp.inf) & (gm < jnp.inf)
    bias = jnp.where(sel, 0.0, MASK_NEG)

    pad = jnp.zeros((LANES - ts, D_ATTN), F32)
    k_new = jnp.concatenate([kn, pad], axis=0).astype(BF16)
    v_new = jnp.concatenate([vn, pad], axis=0).astype(BF16)
    s_new = lax.dot_general(qs, k_new, NT_DIMS, preferred_element_type=F32)
    qi = lax.broadcasted_iota(jnp.int32, (rows, LANES), 0) % ts
    s_new = jnp.where(blk <= qi, s_new, MASK_NEG)

    mx = jnp.max(s_new, axis=-1, keepdims=True)
    for n in range(nb):
        s_list[n] = s_list[n] + bias[:, n:n + 1]
        mx = jnp.maximum(mx, jnp.max(s_list[n], axis=-1, keepdims=True))
    p_new = jnp.exp(s_new - mx)
    den = jnp.sum(p_new, axis=-1, keepdims=True)
    acc = jnp.dot(p_new.astype(BF16), v_new, preferred_element_type=F32)
    for n in range(nb):
        p_n = jnp.exp(s_list[n] - mx)
        den = den + jnp.sum(p_n, axis=-1, keepdims=True)
        vt = block(vp, n).astype(BF16)
        acc = acc + lax.dot_general(p_n.astype(BF16), vt, NT_DIMS, preferred_element_type=F32)
    acc = acc / den
    out = jnp.zeros((ts, D_ATTN), F32)
    for h in range(N_HEADS):
        out = out + jnp.where(lane_head == h, acc[h * ts:(h + 1) * ts, :], 0.0)
    return out


def _attn_sample(q, k_new, v_new, cache_kt, cache_vt, page_table, layer):
    b, ts, _ = q.shape
    n_pages = page_table.shape[1]
    page = cache_kt.shape[3]
    past_len = n_pages * page
    assert ts == SUBLANES and MOBA_BLOCK % page == 0 and past_len % MOBA_BLOCK == 0
    nb = past_len // MOBA_BLOCK
    assert nb <= LANES
    spb = SAMPLES_PER_STEP if b % SAMPLES_PER_STEP == 0 else 1
    tok = pl.BlockSpec((spb, ts, D_ATTN), lambda bi, pt: (bi, 0, 0))

    def page_spec(s, j):
        return pl.BlockSpec((1, 1, D_ATTN, page), lambda bi, pt: (pt[bi * spb + s, j], layer, 0, 0))

    pages, caches = [], []
    for s in range(spb):
        pages += [page_spec(s, j) for j in range(n_pages)] * 2
        caches += [cache_kt] * n_pages + [cache_vt] * n_pages
    return pl.pallas_call(
        functools.partial(_attn_sample_kernel, n_pages=n_pages, k_top=min(MOBA_TOP_K, nb)),
        grid_spec=pltpu.PrefetchScalarGridSpec(
            num_scalar_prefetch=1,
            grid=(b // spb,),
            in_specs=[tok, tok, tok] + pages,
            out_specs=tok,
        ),
        out_shape=jax.ShapeDtypeStruct((b, ts, D_ATTN), F32),
        compiler_params=_params("parallel"),
    )(page_table, q, k_new, v_new, *caches)


def _lane_windows(shape):
    grp = lax.broadcasted_iota(jnp.int32, shape, len(shape) - 1) // (D_POOL // POOL_GROUPS)
    win = jnp.full(shape, POOL_WINDOWS[0], jnp.int32)
    for gi in range(1, POOL_GROUPS):
        win = jnp.where(grp == gi, POOL_WINDOWS[gi], win)
    return win


def _prompt_branches(ti, ug_ref, vn_ref, xc_ref, prev_ref, ws_ref, bs_ref, wp_ref, ps_ref):
    tm = ug_ref.shape[0]
    dg = D_GMLP // GMLP_GROUPS

    wr = lax.broadcasted_iota(jnp.int32, (GMLP_CHUNK, GMLP_GROUPS * GMLP_CHUNK), 0)
    wc = lax.broadcasted_iota(jnp.int32, (GMLP_CHUNK, GMLP_GROUPS * GMLP_CHUNK), 1) % GMLP_CHUNK
    w_tril = jnp.where(wc <= wr, ws_ref[...], 0.0).astype(BF16)
    lane_grp = lax.broadcasted_iota(jnp.int32, (GMLP_CHUNK, D_GMLP), 1) // dg
    b_chunks = []
    for c in range(tm // GMLP_CHUNK):
        rows = slice(c * GMLP_CHUNK, (c + 1) * GMLP_CHUNK)
        vc = vn_ref[rows, :]
        v_stack = jnp.concatenate([jnp.where(lane_grp == gi, vc, 0.0) for gi in range(GMLP_GROUPS)],
                                  axis=0).astype(BF16)
        mixed = jnp.dot(w_tril, v_stack, preferred_element_type=F32) + bs_ref[...]
        b_chunks.append(ug_ref[rows, :] * mixed)
    b_out = jnp.concatenate(b_chunks, axis=0)

    x = xc_ref[...]
    prev = jnp.where(ti > 0, prev_ref[...], 0.0)
    xe = jnp.concatenate([prev, x], axis=0)
    s2 = xe + pltpu.roll(xe, 1, axis=0)
    s4 = s2 + pltpu.roll(s2, 2, axis=0)
    s8 = s4 + pltpu.roll(s4, 4, axis=0)
    s16 = s8 + pltpu.roll(s8, 8, axis=0)
    win = _lane_windows((tm, D_POOL))
    pre = POOL_PREV_ROWS
    wsum = jnp.where(win == 2, s2[pre:], jnp.where(win == 4, s4[pre:],
                     jnp.where(win == 8, s8[pre:], s16[pre:])))
    pos = ti * tm + lax.broadcasted_iota(jnp.int32, (tm, D_POOL), 0)
    count = jnp.minimum(win, pos + 1).astype(F32)
    pooled = wsum / count - x
    y = jnp.dot(pooled.astype(BF16), wp_ref[...], preferred_element_type=F32)
    return b_out, y * ps_ref[...]


def _branch_sample_kernel(ug_ref, vn_ref, xc_ref, pre_ref, wc_ref, bs_ref, wp_ref, ps_ref,
                          b_ref, c_ref, *, pos0):
    nb, ts, _ = ug_ref.shape
    vn = vn_ref[...]
    t_idx = lax.broadcasted_iota(jnp.int32, (ts, D_GMLP), 0)
    mixed = jnp.zeros((nb, ts, D_GMLP), F32) + bs_ref[...][None]
    for s in range(ts):
        coef = jnp.where(s <= t_idx, wc_ref[s], 0.0)
        mixed = mixed + coef[None] * vn[:, s:s + 1, :]
    b_ref[...] = ug_ref[...] * mixed

    x = xc_ref[...]
    pre = pre_ref[...]
    win = _lane_windows((ts, D_POOL))
    end = POOL_BUF + t_idx
    wsum = jnp.zeros((nb, ts, D_POOL), F32)
    for r in range(POOL_BUF + ts):
        row = pre[:, r:r + 1, :] if r < POOL_BUF else x[:, r - POOL_BUF:r - POOL_BUF + 1, :]
        inside = (r <= end) & (r > end - win)
        wsum = wsum + jnp.where(inside, 1.0, 0.0)[None] * row
    count = jnp.minimum(win, pos0 + t_idx + 1).astype(F32)
    pooled = wsum / count[None] - x
    y = jnp.dot(pooled.reshape(nb * ts, D_POOL).astype(BF16), wp_ref[...], preferred_element_type=F32)
    c_ref[...] = (y * ps_ref[...]).reshape(nb, ts, D_POOL)


def _branch_sample(ug, vn, xc, prefix, wc, bs_t, wp_bd, ps, layer, pos0):
    b, ts, _ = ug.shape
    assert ts == SUBLANES
    full = lambda shape: pl.BlockSpec(shape, lambda i: (0,) * len(shape))
    return pl.pallas_call(
        functools.partial(_branch_sample_kernel, pos0=pos0),
        grid=(1,),
        in_specs=[full(ug.shape), full(vn.shape), full(xc.shape), full(prefix.shape)]
        + [_layer_resident(w.shape, layer) for w in (wc, bs_t, wp_bd, ps)],
        out_specs=[full(ug.shape), full(xc.shape)],
        out_shape=[jax.ShapeDtypeStruct(ug.shape, F32), jax.ShapeDtypeStruct(xc.shape, F32)],
        compiler_params=_params("arbitrary"),
    )(ug, vn, xc, prefix, wc, bs_t, wp_bd, ps)


def _merge_mlp_kernel(h_ref, a_ref, *refs, final, ff_chunk, n_branch_refs):
    branch_refs = refs[:n_branch_refs]
    gates_ref, wa_ref, wb_ref, wc_ref, wo_ref, g_ref, wu_ref, wd_ref, gf_ref, o_ref = refs[n_branch_refs:]
    if n_branch_refs == 2:
        b_out, c_out = branch_refs[0][...], branch_refs[1][...]
    else:
        b_out, c_out = _prompt_branches(pl.program_id(1), *branch_refs)

    def branch(x, w_ref, j):
        y = jnp.dot(x.astype(BF16), w_ref[...], preferred_element_type=F32)
        return gates_ref[:, j * D_MODEL:(j + 1) * D_MODEL].astype(F32) * y

    merged = branch(a_ref[...], wa_ref, 0) + branch(b_out, wb_ref, 1) + branch(c_out, wc_ref, 2)
    h1 = h_ref[...] + jnp.dot(merged.astype(BF16), wo_ref[...], preferred_element_type=F32)
    xb = _rms(h1, g_ref[...]).astype(BF16)
    acc = h1
    for c in range(D_FF // ff_chunk):
        cols = slice(c * ff_chunk, (c + 1) * ff_chunk)
        up = jnp.dot(xb, wu_ref[:, cols], preferred_element_type=F32)
        act = jnp.square(jnp.maximum(up, 0.0)).astype(BF16)
        acc = acc + jnp.dot(act, wd_ref[cols, :], preferred_element_type=F32)
    o_ref[...] = _rms(acc, gf_ref[...]) if final else acc


def _merge_mlp(h, a, branches, gates, wa, wb, wc, wo, g, wu, wd, gf, layer, final, nseq, tm):
    n = h.shape[0]
    nt = n // nseq // tm
    row = lambda w: pl.BlockSpec((tm, w), lambda bi, ti: (bi * nt + ti, 0))
    stack = lambda w: _layer_resident(w.shape, layer)
    if len(branches) == 2:
        branch_specs, branch_ops = [row(D_GMLP), row(D_POOL)], list(branches)
    else:
        ug, vn, xc, ws_cat, bs_t, wp_bd, ps = branches
        assert tm % GMLP_CHUNK == 0 and tm % POOL_PREV_ROWS == 0
        per = tm // POOL_PREV_ROWS
        prev = pl.BlockSpec((POOL_PREV_ROWS, D_POOL),
                            lambda bi, ti: (jnp.maximum((bi * nt + ti) * per - 1, 0), 0))
        branch_specs = [row(D_GMLP), row(D_GMLP), row(D_POOL), prev] + [
            _layer_resident(w.shape, layer) for w in (ws_cat, bs_t, wp_bd, ps)]
        branch_ops = [ug, vn, xc, xc, ws_cat, bs_t, wp_bd, ps]
    return pl.pallas_call(
        functools.partial(_merge_mlp_kernel, final=final, ff_chunk=1024,
                          n_branch_refs=len(branch_ops)),
        grid=(nseq, nt),
        in_specs=[row(D_MODEL), row(D_ATTN)] + branch_specs + [
            row(3 * D_MODEL), stack(wa), stack(wb), stack(wc), stack(wo), stack(g),
            stack(wu), stack(wd), _resident((1, D_MODEL))],
        out_specs=row(D_MODEL),
        out_shape=jax.ShapeDtypeStruct((n, D_MODEL), F32),
        compiler_params=_params("parallel", "parallel"),
    )(h, a, *branch_ops, gates, wa, wb, wc, wo, g, wu, wd, gf)


def _block_diag(w):
    n, g, d, e = w.shape
    eye = jnp.eye(g, dtype=w.dtype)
    return (eye[None, :, None, :, None] * w[:, :, :, None, :]).reshape(n, g * d, g * e)


def _pages_channel_major(cache):
    n_phys, depth, page, _, _ = cache.shape
    return jnp.transpose(cache, (0, 1, 3, 4, 2)).reshape(n_phys, depth, D_ATTN, page)


def _heads_last(kt):
    b, depth, _, t = kt.shape
    return jnp.transpose(kt.reshape(b, depth, N_HEADS, HEAD_DIM, t), (0, 1, 4, 2, 3))


def kernel(x_prompt, x_sample, cache_k, cache_v, state_pool, page_table, norm1_g, w_in, gmlp_norm_g,
           w_spatial, b_spatial, w_pool, pool_scale, w_branch_a, w_branch_b, w_branch_c, w_out,
           norm2_g, w_up, w_down, final_norm_g):
    bp, tp, _ = x_prompt.shape
    bs, ts, _ = x_sample.shape
    depth = w_in.shape[0]
    past_len = page_table.shape[1] * cache_k.shape[2]
    dg = D_GMLP // GMLP_GROUPS
    tm = min(512, tp)
    tms = min(512, bs * ts)

    ckt = _pages_channel_major(cache_k)
    cvt = _pages_channel_major(cache_v)
    w_in_b = w_in.astype(BF16)
    wkv_t = _kv_weights_transposed(w_in)
    wa_b, wb_b, wc_b = w_branch_a.astype(BF16), w_branch_b.astype(BF16), w_branch_c.astype(BF16)
    wo_b, wu_b, wd_b = w_out.astype(BF16), w_up.astype(BF16), w_down.astype(BF16)
    gf = final_norm_g.reshape(1, D_MODEL)
    g1 = norm1_g.reshape(depth, 1, D_MODEL)
    g2 = norm2_g.reshape(depth, 1, D_MODEL)
    gn = gmlp_norm_g.reshape(depth, 1, D_GMLP)
    ps = pool_scale.reshape(depth, 1, D_POOL)
    wp_bd = _block_diag(w_pool).astype(BF16)
    ws_cat = jnp.transpose(w_spatial, (0, 2, 1, 3)).reshape(depth, GMLP_CHUNK, GMLP_GROUPS * GMLP_CHUNK)
    bs_full = jnp.repeat(jnp.transpose(b_spatial, (0, 2, 1)), dg, axis=2)
    wc_s = jnp.repeat(jnp.transpose(w_spatial[:, :, :ts, :ts], (0, 3, 2, 1)), dg, axis=3)
    bs_s = bs_full[:, :ts]

    hp = x_prompt.reshape(bp * tp, D_MODEL)
    hs = x_sample.reshape(bs * ts, D_MODEL)
    outs = {n: [] for n in ("pp", "ks", "vs", "ps", "gv")}
    kv_stacks = (jax.ShapeDtypeStruct((bp, depth, D_ATTN, tp), F32),) * 2
    for l in range(depth):
        q, kst, vst, ug, vn, xc, gates = _proj(hp, g1, w_in_b, wkv_t, gn, l, bp, tm, kv_stacks)
        kv_stacks = (kst, vst)
        r3 = lambda a: a.reshape(bp, tp, a.shape[-1])
        a_out = _attn_prompt(r3(q), kst, vst, l)
        hp = _merge_mlp(hp, a_out.reshape(bp * tp, D_ATTN), (ug, vn, xc, ws_cat, bs_full, wp_bd, ps),
                        gates, wa_b, wb_b, wc_b, wo_b, g2, wu_b, wd_b, gf, l, l == depth - 1, bp, tm)
        outs["pp"].append(r3(xc)[:, tp - POOL_BUF:])

        q, k, v, ug, vn, xc, gates = _proj(hs, g1, w_in_b, wkv_t, gn, l, 1, tms)
        s3 = lambda a: a.reshape(bs, ts, a.shape[-1])
        a_out = _attn_sample(s3(q), s3(k), s3(v), ckt, cvt, page_table, l)
        prefix = state_pool[:, l]
        b_out, c_out = _branch_sample(s3(ug), s3(vn), s3(xc), prefix, wc_s, bs_s, wp_bd, ps, l, past_len)
        hs = _merge_mlp(hs, a_out.reshape(bs * ts, D_ATTN),
                        (b_out.reshape(bs * ts, D_GMLP), c_out.reshape(bs * ts, D_POOL)),
                        gates, wa_b, wb_b, wc_b, wo_b, g2, wu_b, wd_b, gf, l, l == depth - 1, 1, tms)
        outs["ks"].append(k.reshape(bs, ts, N_HEADS, HEAD_DIM))
        outs["vs"].append(v.reshape(bs, ts, N_HEADS, HEAD_DIM))
        outs["ps"].append(jnp.concatenate([prefix, s3(xc)], axis=1)[:, -POOL_BUF:])
        outs["gv"].append(s3(vn))

    stack = lambda n: jnp.stack(outs[n], axis=1)
    return (hp.reshape(bp, tp, D_MODEL), hs.reshape(bs, ts, D_MODEL), _heads_last(kv_stacks[0]),
            _heads_last(kv_stacks[1]), stack("pp"), stack("ks"), stack("vs"), stack("ps"), stack("gv"))
```
